```python
import math
import jax, jax.numpy as jnp
from jax import lax
import numpy as np

D_MODEL = 1024
BATCH = 2
SEQ = 16384
DEPTH = 1
DEC_BATCH = 32
DEC_SEQ = 2048
PAST_LEN = 128

HYENA_WIDTH = 512
ATTN_WIDTH = 512
N_DIFF_HEADS = 4
DIFF_HEAD_DIM = 64
DIFF_V_DIM = 2 * DIFF_HEAD_DIM
ROT_DIM = DIFF_HEAD_DIM // 4
ROPE_THETA = 500000.0
D_IN = 3 * HYENA_WIDTH + 3 * ATTN_WIDTH
D_FF = 4 * D_MODEL
FILTER_EMB = 33
FILTER_ORDER = 64
FAST_DECAY_PCT = 0.3
SLOW_DECAY_PCT = 1.5
DECAY_TARGET = 1e-2
Q_BLOCK = 128
NORM_EPS = 1e-6
SUBLN_EPS = 1e-5

kernel_name = "hymba_hyena_diffattn_adaln_encoder"


def rms_norm(x, w, eps=NORM_EPS):
    xf = x.astype(jnp.float32)
    y = xf * lax.rsqrt(jnp.mean(xf * xf, axis=-1, keepdims=True) + eps)
    return (y * w.astype(jnp.float32)).astype(x.dtype)


def implicit_filter(L, w1, b1, w2, b2, w3, b3, w4, freq):
    f32 = jnp.float32
    C = HYENA_WIDTH
    t = jnp.linspace(0.0, 1.0, L, dtype=f32)[:, None]
    bands = (FILTER_EMB - 1) // 2
    w = (2.0 * math.pi) * jnp.arange(L, dtype=f32)[:, None] / L
    f = jnp.linspace(1e-4, bands - 1, bands, dtype=f32)[None, :]
    z = jnp.concatenate([t, jnp.cos(f * w), -jnp.sin(f * w)], axis=-1)
    fr = freq.astype(f32)
    h = jnp.sin(fr * (z @ w1.astype(f32) + b1.astype(f32)))
    h = jnp.sin(fr * (h @ w2.astype(f32) + b2.astype(f32)))
    h = jnp.sin(fr * (h @ w3.astype(f32) + b3.astype(f32)))
    h = h @ w4.astype(f32)
    min_decay = math.log(DECAY_TARGET) / SLOW_DECAY_PCT
    max_decay = math.log(DECAY_TARGET) / FAST_DECAY_PCT
    deltas = jnp.linspace(min_decay, max_decay, C, dtype=f32)
    decay = jnp.exp(-t * jnp.abs(deltas)[None, :])
    h_f = h[:, :C] * decay
    h_b = h[:, C:] * decay
    return jnp.concatenate([h_f.at[0].add(h_b[0]), jnp.zeros((1, C), f32), h_b[:0:-1]], axis=0)


def hyena_mixer(u, conv_w, conv_b, filt_w1, filt_b1, filt_w2, filt_b2, filt_w3, filt_b3, filt_w4,
                filt_freq, hyena_bias, hyena_norm_w):
    B, L, _ = u.shape
    up = jnp.pad(u, ((0, 0), (1, 1), (0, 0)))
    u = up[:, :L] * conv_w[0] + up[:, 1:L + 1] * conv_w[1] + up[:, 2:] * conv_w[2] + conv_b
    x0, x1, v = jnp.split(u, 3, axis=-1)
    s = (x1 * v).astype(jnp.float32)
    k = implicit_filter(L, filt_w1, filt_b1, filt_w2, filt_b2, filt_w3, filt_b3, filt_w4, filt_freq)
    n = 2 * L
    y = jnp.fft.irfft(jnp.fft.rfft(s, n=n, axis=1) * jnp.fft.rfft(k, n=n, axis=0)[None], n=n, axis=1)[:, :L]
    y = x0.astype(jnp.float32) * (y + s * hyena_bias.astype(jnp.float32))
    return rms_norm(y, hyena_norm_w).astype(u.dtype)


def partial_rotary(x):
    L = x.shape[1]
    xf = x.astype(jnp.float32)
    inv_freq = ROPE_THETA ** (-jnp.arange(0, ROT_DIM, 2, dtype=jnp.float32) / ROT_DIM)
    ang = jnp.arange(L, dtype=jnp.float32)[:, None] * inv_freq[None, :]
    ang = jnp.concatenate([ang, ang], axis=-1)[None, :, None, :]
    xr, xp = xf[..., :ROT_DIM], xf[..., ROT_DIM:]
    x1, x2 = xr[..., :ROT_DIM // 2], xr[..., ROT_DIM // 2:]
    xr = xr * jnp.cos(ang) + jnp.concatenate([-x2, x1], axis=-1) * jnp.sin(ang)
    return jnp.concatenate([xr, xp], axis=-1)


def diff_attention(q, k, v, lambda_q1, lambda_k1, lambda_q2, lambda_k2, subln_w, lambda_init):
    B, L, _ = q.shape
    H, d, dv = N_DIFF_HEADS, DIFF_HEAD_DIM, DIFF_V_DIM
    q = partial_rotary(q.reshape(B, L, 2 * H, d)) * (d ** -0.5)
    k = partial_rotary(k.reshape(B, L, 2 * H, d))
    kh = k.transpose(0, 2, 1, 3)
    vh = v.reshape(B, L, H, dv).transpose(0, 2, 1, 3).astype(jnp.float32)
    f32 = jnp.float32
    lam = (jnp.exp(jnp.sum(lambda_q1.astype(f32) * lambda_k1.astype(f32)))
           - jnp.exp(jnp.sum(lambda_q2.astype(f32) * lambda_k2.astype(f32))) + lambda_init)
    nblk = L // Q_BLOCK
    qb = q.transpose(0, 2, 1, 3).reshape(B, 2 * H, nblk, Q_BLOCK, d).transpose(2, 0, 1, 3, 4)

    def block(qblk):
        s = jnp.einsum('bhqd,bhkd->bhqk', qblk, kh)
        p = jax.nn.softmax(s, axis=-1).reshape(B, H, 2, Q_BLOCK, L)
        a = p[:, :, 0] - lam * p[:, :, 1]
        return jnp.einsum('bhqk,bhkd->bhqd', a, vh)

    o = lax.map(block, qb)
    o = o.transpose(1, 0, 3, 2, 4).reshape(B, L, H, dv)
    o = rms_norm(o, subln_w, SUBLN_EPS) * (1.0 - lambda_init)
    return o.reshape(B, L, H * dv)


def encoder(x, c, w_ada, b_ada, norm1_w, w_in, conv_w, conv_b, filt_w1, filt_b1, filt_w2, filt_b2,
            filt_w3, filt_b3, filt_w4, filt_freq, hyena_bias, hyena_norm_w, lambda_q1, lambda_k1,
            lambda_q2, lambda_k2, subln_w, w_out, norm2_w, w_mlp1, w_mlp2, final_w):
    HW3 = 3 * HYENA_WIDTH
    for l in range(DEPTH):
        lambda_init = 0.8 - 0.6 * math.exp(-0.3 * l)
        mod = (jax.nn.silu(c) @ w_ada[l] + b_ada[l])[:, None, :]
        sh1, sc1, g1, sh2, sc2, g2 = jnp.split(mod, 6, axis=-1)
        h = rms_norm(x, norm1_w[l]) * (1.0 + sc1) + sh1
        p = h @ w_in[l]
        hy = hyena_mixer(p[..., :HW3], conv_w[l], conv_b[l], filt_w1[l], filt_b1[l], filt_w2[l],
                         filt_b2[l], filt_w3[l], filt_b3[l], filt_w4[l], filt_freq[l],
                         hyena_bias[l], hyena_norm_w[l])
        q = p[..., HW3:HW3 + ATTN_WIDTH]
        k = p[..., HW3 + ATTN_WIDTH:HW3 + 2 * ATTN_WIDTH]
        v = p[..., HW3 + 2 * ATTN_WIDTH:]
        at = diff_attention(q, k, v, lambda_q1[l], lambda_k1[l], lambda_q2[l], lambda_k2[l],
                            subln_w[l], lambda_init).astype(x.dtype)
        x = x + g1 * (jnp.concatenate([hy, at], axis=-1) @ w_out[l])
        h = rms_norm(x, norm2_w[l]) * (1.0 + sc2) + sh2
        x = x + g2 * (jnp.square(jax.nn.relu(h @ w_mlp1[l])) @ w_mlp2[l])
    return rms_norm(x, final_w)


def setup_inputs(seed: int = 0) -> dict:
    key = jax.random.key(seed)
    ks = jax.random.split(key, 32)
    f32 = jnp.float32
    nrm = lambda k, shape, s: jax.random.normal(k, shape, f32) * s
    D, L_ = D_MODEL, DEPTH
    return {
        "x_prompt": nrm(ks[0], (BATCH, SEQ, D), 1.0),
        "x_sample": nrm(ks[1], (DEC_BATCH, DEC_SEQ, D), 1.0),
        "c_prompt": nrm(ks[2], (BATCH, D), 1.0),
        "c_sample": nrm(ks[3], (DEC_BATCH, D), 1.0),
        "w_ada": nrm(ks[4], (L_, D, 6 * D), 0.5 * D ** -0.5),
        "b_ada": nrm(ks[5], (L_, 6 * D), 0.02),
        "norm1_w": 1.0 + nrm(ks[6], (L_, D), 0.02),
        "w_in": nrm(ks[7], (L_, D, D_IN), D ** -0.5),
        "conv_w": nrm(ks[8], (L_, 3, 3 * HYENA_WIDTH), 3 ** -0.5),
        "conv_b": nrm(ks[9], (L_, 3 * HYENA_WIDTH), 0.02),
        "filt_w1": nrm(ks[10], (L_, FILTER_EMB, FILTER_ORDER), FILTER_EMB ** -0.5),
        "filt_b1": nrm(ks[11], (L_, FILTER_ORDER), 0.1),
        "filt_w2": nrm(ks[12], (L_, FILTER_ORDER, FILTER_ORDER), FILTER_ORDER ** -0.5),
        "filt_b2": nrm(ks[13], (L_, FILTER_ORDER), 0.1),
        "filt_w3": nrm(ks[14], (L_, FILTER_ORDER, FILTER_ORDER), FILTER_ORDER ** -0.5),
        "filt_b3": nrm(ks[15], (L_, FILTER_ORDER), 0.1),
        "filt_w4": nrm(ks[16], (L_, FILTER_ORDER, 2 * HYENA_WIDTH), FILTER_ORDER ** -0.5),
        "filt_freq": 1.0 + nrm(ks[17], (L_, FILTER_ORDER), 0.1),
        "hyena_bias": nrm(ks[18], (L_, HYENA_WIDTH), 1.0),
        "hyena_norm_w": 1.0 + nrm(ks[19], (L_, HYENA_WIDTH), 0.02),
        "lambda_q1": nrm(ks[20], (L_, DIFF_HEAD_DIM), 0.1),
        "lambda_k1": nrm(ks[21], (L_, DIFF_HEAD_DIM), 0.1),
        "lambda_q2": nrm(ks[22], (L_, DIFF_HEAD_DIM), 0.1),
        "lambda_k2": nrm(ks[23], (L_, DIFF_HEAD_DIM), 0.1),
        "subln_w": 1.0 + nrm(ks[24], (L_, DIFF_V_DIM), 0.02),
        "w_out": nrm(ks[25], (L_, D, D), D ** -0.5),
        "norm2_w": 1.0 + nrm(ks[26], (L_, D), 0.02),
        "w_mlp1": nrm(ks[27], (L_, D, D_FF), D ** -0.5),
        "w_mlp2": nrm(ks[28], (L_, D_FF, D), D_FF ** -0.5),
        "final_w": 1.0 + nrm(ks[29], (D,), 0.02),
    }


def reference(x_prompt, x_sample, c_prompt, c_sample, w_ada, b_ada, norm1_w, w_in, conv_w, conv_b,
              filt_w1, filt_b1, filt_w2, filt_b2, filt_w3, filt_b3, filt_w4, filt_freq, hyena_bias,
              hyena_norm_w, lambda_q1, lambda_k1, lambda_q2, lambda_k2, subln_w, w_out, norm2_w,
              w_mlp1, w_mlp2, final_w):
    y_prompt = encoder(x_prompt, c_prompt, w_ada, b_ada, norm1_w, w_in, conv_w, conv_b, filt_w1, filt_b1,
                       filt_w2, filt_b2, filt_w3, filt_b3, filt_w4, filt_freq, hyena_bias, hyena_norm_w,
                       lambda_q1, lambda_k1, lambda_q2, lambda_k2, subln_w, w_out, norm2_w, w_mlp1,
                       w_mlp2, final_w)
    y_sample = encoder(x_sample, c_sample, w_ada, b_ada, norm1_w, w_in, conv_w, conv_b, filt_w1, filt_b1,
                       filt_w2, filt_b2, filt_w3, filt_b3, filt_w4, filt_freq, hyena_bias, hyena_norm_w,
                       lambda_q1, lambda_k1, lambda_q2, lambda_k2, subln_w, w_out, norm2_w, w_mlp1,
                       w_mlp2, final_w)
    return (y_prompt, y_sample)
```

```python
import functools
import math

import numpy as np
import jax
import jax.numpy as jnp
from jax import lax
from jax.experimental import pallas as pl
from jax.experimental.pallas import tpu as pltpu

F32 = jnp.float32
BF16 = jnp.bfloat16

D_MODEL = 1024
HYENA_WIDTH = 512
ATTN_WIDTH = 512
N_DIFF_HEADS = 4
DIFF_HEAD_DIM = 64
DIFF_V_DIM = 2 * DIFF_HEAD_DIM
ROT_DIM = DIFF_HEAD_DIM // 4
ROPE_THETA = 500000.0
D_IN = 3 * HYENA_WIDTH + 3 * ATTN_WIDTH
D_FF = 4 * D_MODEL
FILTER_EMB = 33
FILTER_ORDER = 64
FAST_DECAY_PCT = 0.3
SLOW_DECAY_PCT = 1.5
DECAY_TARGET = 1e-2
NORM_EPS = 1e-6
SUBLN_EPS = 1e-5
LAMBDA_INIT = 0.8 - 0.6 * math.exp(-0.3 * 0)

LANES = 128
SUBLANES = 8
VMEM_LIMIT = 56 * 1024 * 1024

HIGHEST = lax.Precision.HIGHEST


def _cparams(sem):
    return pltpu.CompilerParams(dimension_semantics=sem, vmem_limit_bytes=VMEM_LIMIT)


def _const_spec(shape):
    nd = len(shape)
    return pl.BlockSpec(shape, lambda *_: (0,) * nd, pipeline_mode=pl.Buffered(1))


def _ada_kernel(c_ref, w_ref, b_ref, o_ref):
    c = c_ref[...]
    s = c * jax.nn.sigmoid(c)
    o_ref[...] = jnp.dot(s.astype(BF16), w_ref[...].astype(BF16), preferred_element_type=F32) + b_ref[...]


def _ada(c, w_ada, b_ada):
    nb_, d = c.shape
    n = w_ada.shape[1]
    tn = 1024
    return pl.pallas_call(
        _ada_kernel,
        grid=(n // tn,),
        in_specs=[pl.BlockSpec((nb_, d), lambda j: (0, 0)),
                  pl.BlockSpec((d, tn), lambda j: (0, j)),
                  pl.BlockSpec((1, tn), lambda j: (0, j))],
        out_specs=pl.BlockSpec((nb_, tn), lambda j: (0, j)),
        out_shape=jax.ShapeDtypeStruct((nb_, n), F32),
        compiler_params=_cparams(("parallel",)),
        name="ada",
    )(c, w_ada, b_ada.reshape(1, n))


def _inproj_kernel(x_ref, mod_ref, nw_ref, w_ref, cos_ref, sa_ref, sb_ref, u_ref, q_ref, k_ref, v_ref):
    x = x_ref[0]
    sh = mod_ref[0, 0:1, :]
    sc = mod_ref[0, 1:2, :]
    ms = jnp.mean(x * x, axis=-1, keepdims=True)
    h = (x * lax.rsqrt(ms + NORM_EPS) * nw_ref[...]) * (1.0 + sc) + sh
    hb = h.astype(BF16)
    hw3 = 3 * HYENA_WIDTH
    for c0 in range(0, hw3, 512):
        u_ref[0, :, c0:c0 + 512] = jnp.dot(hb, w_ref[:, c0:c0 + 512], preferred_element_type=F32)
    cos_t = cos_ref[...]
    sa = sa_ref[...]
    sb = sb_ref[...]

    def rope(xg):
        return xg * cos_t + pltpu.roll(xg, LANES - ROT_DIM // 2, axis=1) * sa + pltpu.roll(xg, ROT_DIM // 2, axis=1) * sb

    pq = jnp.dot(hb, w_ref[:, hw3:hw3 + ATTN_WIDTH], preferred_element_type=F32)
    pk = jnp.dot(hb, w_ref[:, hw3 + ATTN_WIDTH:hw3 + 2 * ATTN_WIDTH], preferred_element_type=F32)
    for g in range(ATTN_WIDTH // LANES):
        sl = slice(g * LANES, (g + 1) * LANES)
        q_ref[0, :, sl] = (rope(pq[:, sl]) * (DIFF_HEAD_DIM ** -0.5)).astype(BF16)
        k_ref[0, :, sl] = rope(pk[:, sl]).astype(BF16)
    v_ref[0] = jnp.dot(hb, w_ref[:, hw3 + 2 * ATTN_WIDTH:], preferred_element_type=F32).astype(BF16)


def _inproj(x, mod, norm_w, w_in_bf, cos_t, sa_t, sb_t, tm):
    b, l, d = x.shape
    hw3 = 3 * HYENA_WIDTH
    tok = lambda w: pl.BlockSpec((1, tm, w), lambda i, j: (i, j, 0))
    tab = pl.BlockSpec((tm, LANES), lambda i, j: (j, 0))
    return pl.pallas_call(
        _inproj_kernel,
        grid=(b, l // tm),
        in_specs=[tok(d),
                  pl.BlockSpec((1, 6, d), lambda i, j: (i, 0, 0)),
                  _const_spec((1, d)),
                  _const_spec((d, D_IN)),
                  tab, tab, tab],
        out_specs=[tok(hw3), tok(ATTN_WIDTH), tok(ATTN_WIDTH), tok(ATTN_WIDTH)],
        out_shape=[jax.ShapeDtypeStruct((b, l, hw3), F32),
                   jax.ShapeDtypeStruct((b, l, ATTN_WIDTH), BF16),
                   jax.ShapeDtypeStruct((b, l, ATTN_WIDTH), BF16),
                   jax.ShapeDtypeStruct((b, l, ATTN_WIDTH), BF16)],
        compiler_params=_cparams(("parallel", "parallel")),
        name="inproj",
    )(x, mod, norm_w.reshape(1, d), w_in_bf, cos_t, sa_t, sb_t)


def _rope_tables(l):
    inv_freq = ROPE_THETA ** (-jnp.arange(0, ROT_DIM, 2, dtype=F32) / ROT_DIM)
    ang = jnp.arange(l, dtype=F32)[:, None] * inv_freq[None, :]
    half = ROT_DIM // 2
    zeros = jnp.zeros((l, DIFF_HEAD_DIM - ROT_DIM), F32)
    cos_h = jnp.concatenate([jnp.cos(ang), jnp.cos(ang), jnp.ones_like(zeros)], axis=-1)
    sa_h = jnp.concatenate([-jnp.sin(ang), jnp.zeros((l, half), F32), zeros], axis=-1)
    sb_h = jnp.concatenate([jnp.zeros((l, half), F32), jnp.sin(ang), zeros], axis=-1)
    two = lambda t: jnp.concatenate([t, t], axis=-1)
    return two(cos_h), two(sa_h), two(sb_h)


def _dft_mats(p):
    f = np.arange(p, dtype=np.float64)
    n = np.arange(p, dtype=np.float64)
    th = np.pi * (2.0 * f[:, None] + 1.0) * n[None, :] / (2.0 * p)
    fwd = np.concatenate([np.cos(th), -np.sin(th)], axis=0)
    n2 = np.arange(2 * p, dtype=np.float64)
    th2 = np.pi * (2.0 * f[None, :] + 1.0) * n2[:, None] / (2.0 * p)
    inv = np.concatenate([np.cos(th2), -np.sin(th2)], axis=1) / p
    fwd2 = np.stack([fwd, fwd[:, ::-1]], axis=0).astype(np.float32)
    inv2 = np.stack([inv[:p], inv[p:]], axis=0).astype(np.float32)
    return jnp.asarray(fwd2), jnp.asarray(inv2)


def _filt_kernel(fb_ref, w1t_ref, w1c_ref, w1s_ref, b1_ref, w2_ref, b2_ref, w3_ref, b3_ref, w4h_ref, w4b_ref,
                 fr_ref, ad_ref, mat_ref, o_ref, *, l, p, nb):
    j = pl.program_id(0)
    neg = j < nb
    e = jnp.where(neg, nb - 1 - j, j - nb)
    row = lax.broadcasted_iota(jnp.int32, (p, 1), 0)
    pos = e * p + row + jnp.where(neg, 1, 0)
    posf = pos.astype(F32)
    t = posf / float(l - 1)
    w = (2.0 * math.pi) * posf / float(l)
    ang = w * fb_ref[...]
    fr = fr_ref[...]
    dot = functools.partial(jnp.dot, precision=HIGHEST, preferred_element_type=F32)
    z1 = t * w1t_ref[...] + dot(jnp.cos(ang), w1c_ref[...]) - dot(jnp.sin(ang), w1s_ref[...])
    h = jnp.sin(fr * (z1 + b1_ref[...]))
    h = jnp.sin(fr * (dot(h, w2_ref[...]) + b2_ref[...]))
    h = jnp.sin(fr * (dot(h, w3_ref[...]) + b3_ref[...]))
    kap = dot(h, w4h_ref[...]) * jnp.exp(-t * ad_ref[...])
    kap = jnp.where(pos < l, kap, 0.0)
    hb0 = dot(h[0:1, :], w4b_ref[...])
    kap = kap + jnp.where(jnp.logical_and(row == 0, j == nb), hb0, 0.0)
    o_ref[0] = jnp.dot(mat_ref[0], kap.astype(BF16), preferred_element_type=F32)


def _filter_blocks(l, p, fw1, fb1, fw2, fb2, fw3, fb3, fw4, ffreq, fwd2_bf):
    nb = l // p
    c = HYENA_WIDTH
    bands = (FILTER_EMB - 1) // 2
    fbands = jnp.linspace(1e-4, bands - 1, bands, dtype=F32)[None, :]
    min_decay = math.log(DECAY_TARGET) / SLOW_DECAY_PCT
    max_decay = math.log(DECAY_TARGET) / FAST_DECAY_PCT
    absd = jnp.abs(jnp.linspace(min_decay, max_decay, c, dtype=F32))[None, :]
    half = lambda j: jnp.where(j < nb, 1, 0)
    args = (fbands, fw1[0:1], fw1[1:1 + bands], fw1[1 + bands:], fb1.reshape(1, -1), fw2, fb2.reshape(1, -1),
            fw3, fb3.reshape(1, -1), fw4, fw4[:, c:], ffreq.reshape(1, -1), absd, fwd2_bf)
    specs = [_const_spec(a.shape) for a in args]
    specs[9] = pl.BlockSpec((FILTER_ORDER, c), lambda j: (0, half(j)))
    specs[13] = pl.BlockSpec((1, 2 * p, p), lambda j: (half(j), 0, 0))
    return pl.pallas_call(
        functools.partial(_filt_kernel, l=l, p=p, nb=nb),
        grid=(2 * nb,),
        in_specs=specs,
        out_specs=pl.BlockSpec((1, 2 * p, c), lambda j: (j, 0, 0)),
        out_shape=jax.ShapeDtypeStruct((2 * nb, 2 * p, c), F32),
        compiler_params=_cparams(("parallel",)),
        name="filt",
    )(*args)


def _sigdft_kernel(u_ref, prev_ref, next_ref, cw_ref, cb_ref, fwd_ref, sh_ref, x0_ref, s_ref, *, p, nb):
    a = pl.program_id(1)
    c = HYENA_WIDTH
    u = u_ref[0]
    row = lax.broadcasted_iota(jnp.int32, (p, 1), 0)
    prev_row = jnp.where(a > 0, prev_ref[0, SUBLANES - 1:SUBLANES, :], 0.0)
    next_row = jnp.where(a < nb - 1, next_ref[0, 0:1, :], 0.0)
    up = jnp.where(row == 0, prev_row, pltpu.roll(u, 1, axis=0))
    un = jnp.where(row == p - 1, next_row, pltpu.roll(u, p - 1, axis=0))
    cv = up * cw_ref[0:1, :] + u * cw_ref[1:2, :] + un * cw_ref[2:3, :] + cb_ref[...]
    s = cv[:, c:2 * c] * cv[:, 2 * c:]
    x0_ref[0] = cv[:, :c]
    s_ref[0] = s
    sh_ref[0, 0] = jnp.dot(fwd_ref[0], s.astype(BF16), preferred_element_type=F32)


def _sigdft(u, conv_w, conv_b, fwd2_bf, p):
    b, l, c3 = u.shape
    nb = l // p
    c = HYENA_WIDTH
    r8 = p // SUBLANES
    nrow8 = l // SUBLANES
    return pl.pallas_call(
        functools.partial(_sigdft_kernel, p=p, nb=nb),
        grid=(b, nb),
        in_specs=[pl.BlockSpec((1, p, c3), lambda i, a: (i, a, 0)),
                  pl.BlockSpec((1, SUBLANES, c3), lambda i, a: (i, jnp.maximum(a * r8 - 1, 0), 0)),
                  pl.BlockSpec((1, SUBLANES, c3), lambda i, a: (i, jnp.minimum((a + 1) * r8, nrow8 - 1), 0)),
                  _const_spec((3, c3)), _const_spec((1, c3)),
                  pl.BlockSpec((1, 2 * p, p), lambda i, a: (0, 0, 0))],
        out_specs=[pl.BlockSpec((1, 1, 2 * p, c), lambda i, a: (i, a, 0, 0)),
                   pl.BlockSpec((1, p, c), lambda i, a: (i, a, 0)),
                   pl.BlockSpec((1, p, c), lambda i, a: (i, a, 0))],
        out_shape=[jax.ShapeDtypeStruct((b, nb, 2 * p, c), F32),
                   jax.ShapeDtypeStruct((b, l, c), F32),
                   jax.ShapeDtypeStruct((b, l, c), F32)],
        compiler_params=_cparams(("parallel", "parallel")),
        name="sigdft",
    )(u, u, u, conv_w, conv_b.reshape(1, c3), fwd2_bf)


def _facc_kernel(s_ref, k_ref, z_ref, *, nb):
    bt, _, _, ft, ct = s_ref.shape

    def per_e(e, carry):
        def per_a(a, acc):
            are, aim = acc
            j = e - 1 - a + nb
            kre = k_ref[j, 0][None]
            kim = k_ref[j, 1][None]
            sre = s_ref[:, a, 0]
            sim = s_ref[:, a, 1]
            return (are + (kre * sre - kim * sim), aim + (kre * sim + kim * sre))

        zero = jnp.zeros((bt, ft, ct), F32)
        are, aim = lax.fori_loop(0, nb, per_a, (zero, zero))
        z_ref[:, e, 0] = are
        z_ref[:, e, 1] = aim
        return carry

    lax.fori_loop(0, nb + 1, per_e, 0)


def _facc(sh, kh, p, bt, ft):
    b, nb = sh.shape[0], sh.shape[1]
    c = HYENA_WIDTH
    ct = LANES
    s5 = sh.reshape(b, nb, 2, p, c)
    k4 = kh.reshape(2 * nb, 2, p, c)
    z = pl.pallas_call(
        functools.partial(_facc_kernel, nb=nb),
        grid=(c // ct, p // ft, b // bt),
        in_specs=[pl.BlockSpec((bt, nb, 2, ft, ct), lambda ci, fi, bi: (bi, 0, 0, fi, ci)),
                  pl.BlockSpec((2 * nb, 2, ft, ct), lambda ci, fi, bi: (0, 0, fi, ci))],
        out_specs=pl.BlockSpec((bt, nb + 1, 2, ft, ct), lambda ci, fi, bi: (bi, 0, 0, fi, ci)),
        out_shape=jax.ShapeDtypeStruct((b, nb + 1, 2, p, c), F32),
        compiler_params=_cparams(("parallel", "parallel", "parallel")),
        name="facc",
    )(s5, k4)
    return z.reshape(b, nb + 1, 2 * p, c)


def _hyout_kernel(zc_ref, zp_ref, inv_ref, x0_ref, s_ref, bias_ref, nw_ref, o_ref):
    y = (jnp.dot(inv_ref[0], zc_ref[0, 0].astype(BF16), preferred_element_type=F32)
         + jnp.dot(inv_ref[1], zp_ref[0, 0].astype(BF16), preferred_element_type=F32))
    y = x0_ref[0] * (y + s_ref[0] * bias_ref[...])
    ms = jnp.mean(y * y, axis=-1, keepdims=True)
    o_ref[0] = (y * lax.rsqrt(ms + NORM_EPS) * nw_ref[...]).astype(o_ref.dtype)


def _hyout(zh, inv2_bf, x0, s, hbias, hnorm_w, p):
    b, l, c = x0.shape
    nb = l // p
    tok = pl.BlockSpec((1, p, c), lambda i, a: (i, a, 0))
    return pl.pallas_call(
        _hyout_kernel,
        grid=(b, nb),
        in_specs=[pl.BlockSpec((1, 1, 2 * p, c), lambda i, a: (i, a + 1, 0, 0)),
                  pl.BlockSpec((1, 1, 2 * p, c), lambda i, a: (i, a, 0, 0)),
                  _const_spec((2, p, 2 * p)),
                  tok, tok, _const_spec((1, c)), _const_spec((1, c))],
        out_specs=tok,
        out_shape=jax.ShapeDtypeStruct((b, l, c), BF16),
        compiler_params=_cparams(("parallel", "parallel")),
        name="hyout",
    )(zh, zh, inv2_bf, x0, s, hbias.reshape(1, c), hnorm_w.reshape(1, c))


def _attn_kernel(q_ref, k_ref, v_ref, lq1_ref, lk1_ref, lq2_ref, lk2_ref, sw_ref, o_ref,
                 qs_ref, m_ref, l_ref, acc_ref, *, tq, tk, nk):
    q = q_ref[0]
    lane = lax.broadcasted_iota(jnp.int32, (tq, LANES), 1)
    zero = jnp.zeros_like(q)
    qs_ref[0:tq, :] = jnp.where(lane < DIFF_HEAD_DIM, q, zero)
    qs_ref[tq:2 * tq, :] = jnp.where(lane >= DIFF_HEAD_DIM, q, zero)
    m_ref[...] = jnp.full(m_ref.shape, -jnp.inf, F32)
    l_ref[...] = jnp.zeros(l_ref.shape, F32)
    acc_ref[...] = jnp.zeros(acc_ref.shape, F32)

    def step(i, carry):
        off = pl.multiple_of(i * tk, tk)
        kb = k_ref[0, pl.ds(off, tk), :]
        vb = v_ref[0, pl.ds(off, tk), :]
        s = lax.dot_general(qs_ref[...], kb, (((1,), (1,)), ((), ())), preferred_element_type=F32)
        m_prev = m_ref[...]
        m_new = jnp.maximum(m_prev, jnp.max(s, axis=-1, keepdims=True))
        alpha = jnp.exp(m_prev - m_new)
        pexp = jnp.exp(s - m_new)
        l_ref[...] = alpha * l_ref[...] + jnp.sum(pexp, axis=-1, keepdims=True)
        acc_ref[...] = alpha * acc_ref[...] + jnp.dot(pexp.astype(BF16), vb, preferred_element_type=F32)
        m_ref[...] = m_new
        return carry

    lax.fori_loop(0, nk, step, 0)

    lam = (jnp.exp(jnp.sum(lq1_ref[...] * lk1_ref[...], axis=-1, keepdims=True))
           - jnp.exp(jnp.sum(lq2_ref[...] * lk2_ref[...], axis=-1, keepdims=True)) + LAMBDA_INIT)
    o = acc_ref[...] / l_ref[...]
    o = o[0:tq, :] - lam * o[tq:2 * tq, :]
    ms = jnp.mean(o * o, axis=-1, keepdims=True)
    o = (o * lax.rsqrt(ms + SUBLN_EPS) * sw_ref[...]) * (1.0 - LAMBDA_INIT)
    o_ref[0] = o.astype(o_ref.dtype)


def _attn(q, k, v, lq1, lk1, lq2, lk2, subln_w, tq, tk):
    b, l, _ = q.shape
    nk = l // tk
    vec = lambda a: a.reshape(1, -1)
    kv_spec = pl.BlockSpec((1, l, LANES), lambda i, h, j: (i, 0, h))
    q_spec = pl.BlockSpec((1, tq, LANES), lambda i, h, j: (i, j, h))
    return pl.pallas_call(
        functools.partial(_attn_kernel, tq=tq, tk=tk, nk=nk),
        grid=(b, N_DIFF_HEADS, l // tq),
        in_specs=[q_spec, kv_spec, kv_spec,
                  _const_spec((1, DIFF_HEAD_DIM)), _const_spec((1, DIFF_HEAD_DIM)),
                  _const_spec((1, DIFF_HEAD_DIM)), _const_spec((1, DIFF_HEAD_DIM)),
                  _const_spec((1, DIFF_V_DIM))],
        out_specs=q_spec,
        out_shape=jax.ShapeDtypeStruct((b, l, ATTN_WIDTH), BF16),
        scratch_shapes=[pltpu.VMEM((2 * tq, LANES), BF16),
                        pltpu.VMEM((2 * tq, 1), F32),
                        pltpu.VMEM((2 * tq, 1), F32),
                        pltpu.VMEM((2 * tq, DIFF_V_DIM), F32)],
        compiler_params=_cparams(("parallel", "parallel", "parallel")),
        name="attn",
    )(q, k, v, vec(lq1), vec(lk1), vec(lq2), vec(lk2), vec(subln_w))


def _post_kernel(x_ref, hy_ref, at_ref, mod_ref, woh_ref, woa_ref, n2_ref, w1_ref, w2_ref, fw_ref, o_ref, *, ffc):
    x = x_ref[0]
    g1 = mod_ref[0, 2:3, :]
    sh2 = mod_ref[0, 3:4, :]
    sc2 = mod_ref[0, 4:5, :]
    g2 = mod_ref[0, 5:6, :]
    mix = (jnp.dot(hy_ref[0], woh_ref[...], preferred_element_type=F32)
           + jnp.dot(at_ref[0], woa_ref[...], preferred_element_type=F32))
    x1 = x + g1 * mix
    ms = jnp.mean(x1 * x1, axis=-1, keepdims=True)
    hb = ((x1 * lax.rsqrt(ms + NORM_EPS) * n2_ref[...]) * (1.0 + sc2) + sh2).astype(BF16)
    acc = jnp.zeros_like(x1)
    for c0 in range(0, D_FF, ffc):
        hc = jnp.dot(hb, w1_ref[:, c0:c0 + ffc], preferred_element_type=F32)
        hc = jnp.square(jnp.maximum(hc, 0.0))
        acc = acc + jnp.dot(hc.astype(BF16), w2_ref[c0:c0 + ffc, :], preferred_element_type=F32)
    x2 = x1 + g2 * acc
    ms2 = jnp.mean(x2 * x2, axis=-1, keepdims=True)
    o_ref[0] = x2 * lax.rsqrt(ms2 + NORM_EPS) * fw_ref[...]


def _post(x, hy, at, mod, wo_bf, norm2_w, w1_bf, w2_bf, final_w, tm):
    b, l, d = x.shape
    tok = lambda w: pl.BlockSpec((1, tm, w), lambda i, j: (i, j, 0))
    return pl.pallas_call(
        functools.partial(_post_kernel, ffc=1024),
        grid=(b, l // tm),
        in_specs=[tok(d), tok(HYENA_WIDTH), tok(ATTN_WIDTH),
                  pl.BlockSpec((1, 6, d), lambda i, j: (i, 0, 0)),
                  _const_spec((HYENA_WIDTH, d)), _const_spec((ATTN_WIDTH, d)),
                  _const_spec((1, d)), _const_spec((d, D_FF)), _const_spec((D_FF, d)), _const_spec((1, d))],
        out_specs=tok(d),
        out_shape=jax.ShapeDtypeStruct((b, l, d), F32),
        compiler_params=_cparams(("parallel", "parallel")),
        name="post",
    )(x, hy, at, mod, wo_bf[:HYENA_WIDTH], wo_bf[HYENA_WIDTH:], norm2_w.reshape(1, d), w1_bf, w2_bf,
      final_w.reshape(1, d))


def _tiles(b, l):
    tm = min(512, l)
    p = min(512, l // 2)
    tq = min(256, l)
    tk = min(512, l)
    nb = l // p
    bt = math.gcd(b, 8)
    per_ft = LANES * 4 * 2 * (bt * nb + 2 * nb + bt * (nb + 1))
    ft = SUBLANES
    while ft * 2 <= min(p, 64) and per_ft * ft * 2 <= 8 * 1024 * 1024:
        ft *= 2
    return dict(tm=tm, p=p, tq=tq, tk=tk, bt=bt, ft=ft)


def _encoder(x, mod, w, tables):
    b, l, d = x.shape
    t = _tiles(b, l)
    p = t["p"]
    fwd2_bf, inv2_bf = tables["fwd2"], tables["inv2"]
    cos_t, sa_t, sb_t = _rope_tables(l)
    u, q, k, v = _inproj(x, mod, w["norm1_w"], w["w_in"], cos_t, sa_t, sb_t, t["tm"])
    kh = _filter_blocks(l, p, w["filt_w1"], w["filt_b1"], w["filt_w2"], w["filt_b2"], w["filt_w3"], w["filt_b3"],
                        w["filt_w4"], w["filt_freq"], fwd2_bf)
    sh, x0, s = _sigdft(u, w["conv_w"], w["conv_b"], fwd2_bf, p)
    zh = _facc(sh, kh, p, t["bt"], t["ft"])
    hy = _hyout(zh, inv2_bf, x0, s, w["hyena_bias"], w["hyena_norm_w"], p)
    at = _attn(q, k, v, w["lambda_q1"], w["lambda_k1"], w["lambda_q2"], w["lambda_k2"], w["subln_w"],
               t["tq"], t["tk"])
    return _post(x, hy, at, mod, w["w_out"], w["norm2_w"], w["w_mlp1"], w["w_mlp2"], w["final_w"], t["tm"])


def kernel(x_prompt, x_sample, c_prompt, c_sample, w_ada, b_ada, norm1_w, w_in, conv_w, conv_b, filt_w1, filt_b1, filt_w2, filt_b2, filt_w3, filt_b3, filt_w4, filt_freq, hyena_bias, hyena_norm_w, lambda_q1, lambda_k1, lambda_q2, lambda_k2, subln_w, w_out, norm2_w, w_mlp1, w_mlp2, final_w):
    assert w_ada.shape[0] == 1, "single-layer encoder"
    w = dict(norm1_w=norm1_w[0], w_in=w_in[0].astype(BF16), conv_w=conv_w[0], conv_b=conv_b[0],
             filt_w1=filt_w1[0], filt_b1=filt_b1[0], filt_w2=filt_w2[0], filt_b2=filt_b2[0],
             filt_w3=filt_w3[0], filt_b3=filt_b3[0], filt_w4=filt_w4[0], filt_freq=filt_freq[0],
             hyena_bias=hyena_bias[0], hyena_norm_w=hyena_norm_w[0], lambda_q1=lambda_q1[0],
             lambda_k1=lambda_k1[0], lambda_q2=lambda_q2[0], lambda_k2=lambda_k2[0], subln_w=subln_w[0],
             w_out=w_out[0].astype(BF16), norm2_w=norm2_w[0], w_mlp1=w_mlp1[0].astype(BF16),
             w_mlp2=w_mlp2[0].astype(BF16), final_w=final_w)
    nbp = c_prompt.shape[0]
    mod = _ada(jnp.concatenate([c_prompt, c_sample], axis=0), w_ada[0], b_ada[0])
    mod = mod.reshape(mod.shape[0], 6, D_MODEL)
    outs = []
    for x, m in ((x_prompt, mod[:nbp]), (x_sample, mod[nbp:])):
        p = _tiles(*x.shape[:2])["p"]
        fwd2, inv2 = _dft_mats(p)
        tables = dict(fwd2=fwd2.astype(BF16), inv2=inv2.astype(BF16))
        outs.append(_encoder(x, m, w, tables))
    return tuple(outs)
```

```python
import functools
import math

import numpy as np
import jax
import jax.numpy as jnp
from jax import lax
from jax.experimental import pallas as pl
from jax.experimental.pallas import tpu as pltpu

F32 = jnp.float32
BF16 = jnp.bfloat16

D_MODEL = 1024
HYENA_WIDTH = 512
ATTN_WIDTH = 512
N_DIFF_HEADS = 4
DIFF_HEAD_DIM = 64
DIFF_V_DIM = 2 * DIFF_HEAD_DIM
ROT_DIM = DIFF_HEAD_DIM // 4
ROPE_THETA = 500000.0
D_IN = 3 * HYENA_WIDTH + 3 * ATTN_WIDTH
D_FF = 4 * D_MODEL
FILTER_EMB = 33
FILTER_ORDER = 64
FAST_DECAY_PCT = 0.3
SLOW_DECAY_PCT = 1.5
DECAY_TARGET = 1e-2
NORM_EPS = 1e-6
SUBLN_EPS = 1e-5
LAMBDA_INIT = 0.8 - 0.6 * math.exp(-0.3 * 0)

LANES = 128
SUBLANES = 8
VMEM_LIMIT = 56 * 1024 * 1024

HIGHEST = lax.Precision.HIGHEST


def _cparams(sem):
    return pltpu.CompilerParams(dimension_semantics=sem, vmem_limit_bytes=VMEM_LIMIT)


def _const_spec(shape):
    nd = len(shape)
    return pl.BlockSpec(shape, lambda *_: (0,) * nd, pipeline_mode=pl.Buffered(1))


def _ada_kernel(c_ref, w_ref, b_ref, o_ref):
    c = c_ref[...]
    s = c * jax.nn.sigmoid(c)
    o_ref[...] = jnp.dot(s.astype(BF16), w_ref[...].astype(BF16), preferred_element_type=F32) + b_ref[...]


def _ada(c, w_ada, b_ada):
    nb_, d = c.shape
    n = w_ada.shape[1]
    tn = 1024
    return pl.pallas_call(
        _ada_kernel,
        grid=(n // tn,),
        in_specs=[pl.BlockSpec((nb_, d), lambda j: (0, 0)),
                  pl.BlockSpec((d, tn), lambda j: (0, j)),
                  pl.BlockSpec((1, tn), lambda j: (0, j))],
        out_specs=pl.BlockSpec((nb_, tn), lambda j: (0, j)),
        out_shape=jax.ShapeDtypeStruct((nb_, n), F32),
        compiler_params=_cparams(("parallel",)),
        name="ada",
    )(c, w_ada, b_ada.reshape(1, n))


def _inproj_kernel(x_ref, mod_ref, nw_ref, w_ref, cos_ref, sa_ref, sb_ref, u_ref, q_ref, k_ref, v_ref):
    x = x_ref[0]
    sh = mod_ref[0, 0:1, :]
    sc = mod_ref[0, 1:2, :]
    ms = jnp.mean(x * x, axis=-1, keepdims=True)
    h = (x * lax.rsqrt(ms + NORM_EPS) * nw_ref[...]) * (1.0 + sc) + sh
    hb = h.astype(BF16)
    hw3 = 3 * HYENA_WIDTH
    for c0 in range(0, hw3, 512):
        u_ref[0, :, c0:c0 + 512] = jnp.dot(hb, w_ref[:, c0:c0 + 512], preferred_element_type=F32)
    cos_t = cos_ref[...]
    sa = sa_ref[...]
    sb = sb_ref[...]

    def rope(xg):
        return xg * cos_t + pltpu.roll(xg, LANES - ROT_DIM // 2, axis=1) * sa + pltpu.roll(xg, ROT_DIM // 2, axis=1) * sb

    pq = jnp.dot(hb, w_ref[:, hw3:hw3 + ATTN_WIDTH], preferred_element_type=F32)
    pk = jnp.dot(hb, w_ref[:, hw3 + ATTN_WIDTH:hw3 + 2 * ATTN_WIDTH], preferred_element_type=F32)
    for g in range(ATTN_WIDTH // LANES):
        sl = slice(g * LANES, (g + 1) * LANES)
        q_ref[0, :, sl] = (rope(pq[:, sl]) * (math.log2(math.e) * DIFF_HEAD_DIM ** -0.5)).astype(BF16)
        k_ref[0, :, sl] = rope(pk[:, sl]).astype(BF16)
    pv = jnp.dot(hb, w_ref[:, hw3 + 2 * ATTN_WIDTH:], preferred_element_type=F32).astype(BF16)
    ones = jnp.ones((pv.shape[0], DIFF_V_DIM), BF16)
    for g in range(N_DIFF_HEADS):
        v_ref[0, :, 2 * g * DIFF_V_DIM:(2 * g + 1) * DIFF_V_DIM] = pv[:, g * DIFF_V_DIM:(g + 1) * DIFF_V_DIM]
        v_ref[0, :, (2 * g + 1) * DIFF_V_DIM:(2 * g + 2) * DIFF_V_DIM] = ones


def _inproj(x, mod, norm_w, w_in_bf, cos_t, sa_t, sb_t, tm):
    b, l, d = x.shape
    hw3 = 3 * HYENA_WIDTH
    tok = lambda w: pl.BlockSpec((1, tm, w), lambda i, j: (i, j, 0))
    tab = pl.BlockSpec((tm, LANES), lambda i, j: (j, 0))
    return pl.pallas_call(
        _inproj_kernel,
        grid=(b, l // tm),
        in_specs=[tok(d),
                  pl.BlockSpec((1, 6, d), lambda i, j: (i, 0, 0)),
                  _const_spec((1, d)),
                  _const_spec((d, D_IN)),
                  tab, tab, tab],
        out_specs=[tok(hw3), tok(ATTN_WIDTH), tok(ATTN_WIDTH), tok(2 * ATTN_WIDTH)],
        out_shape=[jax.ShapeDtypeStruct((b, l, hw3), F32),
                   jax.ShapeDtypeStruct((b, l, ATTN_WIDTH), BF16),
                   jax.ShapeDtypeStruct((b, l, ATTN_WIDTH), BF16),
                   jax.ShapeDtypeStruct((b, l, 2 * ATTN_WIDTH), BF16)],
        compiler_params=_cparams(("parallel", "parallel")),
        name="inproj",
    )(x, mod, norm_w.reshape(1, d), w_in_bf, cos_t, sa_t, sb_t)


def _rope_tables(l):
    inv_freq = ROPE_THETA ** (-jnp.arange(0, ROT_DIM, 2, dtype=F32) / ROT_DIM)
    ang = jnp.arange(l, dtype=F32)[:, None] * inv_freq[None, :]
    half = ROT_DIM // 2
    zeros = jnp.zeros((l, DIFF_HEAD_DIM - ROT_DIM), F32)
    cos_h = jnp.concatenate([jnp.cos(ang), jnp.cos(ang), jnp.ones_like(zeros)], axis=-1)
    sa_h = jnp.concatenate([-jnp.sin(ang), jnp.zeros((l, half), F32), zeros], axis=-1)
    sb_h = jnp.concatenate([jnp.zeros((l, half), F32), jnp.sin(ang), zeros], axis=-1)
    two = lambda t: jnp.concatenate([t, t], axis=-1)
    return two(cos_h), two(sa_h), two(sb_h)


def _dft_mats(p):
    f = np.arange(p, dtype=np.float64)
    n = np.arange(p, dtype=np.float64)
    th = np.pi * (2.0 * f[:, None] + 1.0) * n[None, :] / (2.0 * p)
    fwd = np.concatenate([np.cos(th), -np.sin(th)], axis=0)
    n2 = np.arange(2 * p, dtype=np.float64)
    th2 = np.pi * (2.0 * f[None, :] + 1.0) * n2[:, None] / (2.0 * p)
    inv = np.concatenate([np.cos(th2), -np.sin(th2)], axis=1) / p
    fwd2 = np.stack([fwd, fwd[:, ::-1]], axis=0).astype(np.float32)
    inv2 = np.stack([inv[:p], inv[p:]], axis=0).astype(np.float32)
    return jnp.asarray(fwd2), jnp.asarray(inv2)


def _filt_kernel(fb_ref, w1t_ref, w1c_ref, w1s_ref, b1_ref, w2_ref, b2_ref, w3_ref, b3_ref, w4h_ref, w4b_ref,
                 fr_ref, ad_ref, mat_ref, o_ref, *, l, p, nb):
    j = pl.program_id(0)
    neg = j < nb
    e = jnp.where(neg, nb - 1 - j, j - nb)
    row = lax.broadcasted_iota(jnp.int32, (p, 1), 0)
    pos = e * p + row + jnp.where(neg, 1, 0)
    posf = pos.astype(F32)
    t = posf / float(l - 1)
    w = (2.0 * math.pi) * posf / float(l)
    ang = w * fb_ref[...]
    fr = fr_ref[...]
    dot = functools.partial(jnp.dot, precision=HIGHEST, preferred_element_type=F32)
    z1 = t * w1t_ref[...] + dot(jnp.cos(ang), w1c_ref[...]) - dot(jnp.sin(ang), w1s_ref[...])
    h = jnp.sin(fr * (z1 + b1_ref[...]))
    h = jnp.sin(fr * (dot(h, w2_ref[...]) + b2_ref[...]))
    h = jnp.sin(fr * (dot(h, w3_ref[...]) + b3_ref[...]))
    kap = dot(h, w4h_ref[...]) * jnp.exp(-t * ad_ref[...])
    kap = jnp.where(pos < l, kap, 0.0)
    hb0 = dot(h[0:1, :], w4b_ref[...])
    kap = kap + jnp.where(jnp.logical_and(row == 0, j == nb), hb0, 0.0)
    o_ref[0] = jnp.dot(mat_ref[0], kap.astype(BF16), preferred_element_type=F32)


def _filter_blocks(l, p, fw1, fb1, fw2, fb2, fw3, fb3, fw4, ffreq, fwd2_bf):
    nb = l // p
    c = HYENA_WIDTH
    bands = (FILTER_EMB - 1) // 2
    fbands = jnp.linspace(1e-4, bands - 1, bands, dtype=F32)[None, :]
    min_decay = math.log(DECAY_TARGET) / SLOW_DECAY_PCT
    max_decay = math.log(DECAY_TARGET) / FAST_DECAY_PCT
    absd = jnp.abs(jnp.linspace(min_decay, max_decay, c, dtype=F32))[None, :]
    half = lambda j: jnp.where(j < nb, 1, 0)
    args = (fbands, fw1[0:1], fw1[1:1 + bands], fw1[1 + bands:], fb1.reshape(1, -1), fw2, fb2.reshape(1, -1),
            fw3, fb3.reshape(1, -1), fw4, fw4[:, c:], ffreq.reshape(1, -1), absd, fwd2_bf)
    specs = [_const_spec(a.shape) for a in args]
    specs[9] = pl.BlockSpec((FILTER_ORDER, c), lambda j: (0, half(j)))
    specs[13] = pl.BlockSpec((1, 2 * p, p), lambda j: (half(j), 0, 0))
    return pl.pallas_call(
        functools.partial(_filt_kernel, l=l, p=p, nb=nb),
        grid=(2 * nb,),
        in_specs=specs,
        out_specs=pl.BlockSpec((1, 2 * p, c), lambda j: (j, 0, 0)),
        out_shape=jax.ShapeDtypeStruct((2 * nb, 2 * p, c), F32),
        compiler_params=_cparams(("parallel",)),
        name="filt",
    )(*args)


def _sigdft_kernel(u_ref, prev_ref, next_ref, cw_ref, cb_ref, fwd_ref, sh_ref, x0_ref, s_ref, *, p, nb):
    a = pl.program_id(1)
    c = HYENA_WIDTH
    u = u_ref[0]
    row = lax.broadcasted_iota(jnp.int32, (p, 1), 0)
    prev_row = jnp.where(a > 0, prev_ref[0, SUBLANES - 1:SUBLANES, :], 0.0)
    next_row = jnp.where(a < nb - 1, next_ref[0, 0:1, :], 0.0)
    up = jnp.where(row == 0, prev_row, pltpu.roll(u, 1, axis=0))
    un = jnp.where(row == p - 1, next_row, pltpu.roll(u, p - 1, axis=0))
    cv = up * cw_ref[0:1, :] + u * cw_ref[1:2, :] + un * cw_ref[2:3, :] + cb_ref[...]
    s = cv[:, c:2 * c] * cv[:, 2 * c:]
    x0_ref[0] = cv[:, :c]
    s_ref[0] = s
    sh_ref[0, 0] = jnp.dot(fwd_ref[0], s.astype(BF16), preferred_element_type=F32)


def _sigdft(u, conv_w, conv_b, fwd2_bf, p):
    b, l, c3 = u.shape
    nb = l // p
    c = HYENA_WIDTH
    r8 = p // SUBLANES
    nrow8 = l // SUBLANES
    return pl.pallas_call(
        functools.partial(_sigdft_kernel, p=p, nb=nb),
        grid=(b, nb),
        in_specs=[pl.BlockSpec((1, p, c3), lambda i, a: (i, a, 0)),
                  pl.BlockSpec((1, SUBLANES, c3), lambda i, a: (i, jnp.maximum(a * r8 - 1, 0), 0)),
                  pl.BlockSpec((1, SUBLANES, c3), lambda i, a: (i, jnp.minimum((a + 1) * r8, nrow8 - 1), 0)),
                  _const_spec((3, c3)), _const_spec((1, c3)),
                  pl.BlockSpec((1, 2 * p, p), lambda i, a: (0, 0, 0))],
        out_specs=[pl.BlockSpec((1, 1, 2 * p, c), lambda i, a: (i, a, 0, 0)),
                   pl.BlockSpec((1, p, c), lambda i, a: (i, a, 0)),
                   pl.BlockSpec((1, p, c), lambda i, a: (i, a, 0))],
        out_shape=[jax.ShapeDtypeStruct((b, nb, 2 * p, c), F32),
                   jax.ShapeDtypeStruct((b, l, c), F32),
                   jax.ShapeDtypeStruct((b, l, c), F32)],
        compiler_params=_cparams(("parallel", "parallel")),
        name="sigdft",
    )(u, u, u, conv_w, conv_b.reshape(1, c3), fwd2_bf)


def _facc_kernel(s_ref, k_ref, z_ref, *, nb, rt):
    bt, _, _, ft, ct = s_ref.shape

    def per_tile(ti, carry0):
        rows = pl.ds(pl.multiple_of(ti * rt, rt), rt)

        def per_e(e, carry):
            def per_a(a, acc):
                are, aim = acc
                j = e - 1 - a + nb
                kre = k_ref[j, 0, rows, :][None]
                kim = k_ref[j, 1, rows, :][None]
                sre = s_ref[:, a, 0, rows, :]
                sim = s_ref[:, a, 1, rows, :]
                return (are + (kre * sre - kim * sim), aim + (kre * sim + kim * sre))

            zero = jnp.zeros((bt, rt, ct), F32)
            are, aim = lax.fori_loop(0, nb, per_a, (zero, zero), unroll=min(nb, 4))
            z_ref[:, e, 0, rows, :] = are
            z_ref[:, e, 1, rows, :] = aim
            return carry

        lax.fori_loop(0, nb + 1, per_e, 0)
        return carry0

    lax.fori_loop(0, ft // rt, per_tile, 0)


def _facc(sh, kh, p, bt, ft):
    b, nb = sh.shape[0], sh.shape[1]
    c = HYENA_WIDTH
    ct = LANES
    s5 = sh.reshape(b, nb, 2, p, c)
    k4 = kh.reshape(2 * nb, 2, p, c)
    rt = min(ft, (8 * SUBLANES) // bt)
    z = pl.pallas_call(
        functools.partial(_facc_kernel, nb=nb, rt=rt),
        grid=(c // ct, p // ft, b // bt),
        in_specs=[pl.BlockSpec((bt, nb, 2, ft, ct), lambda ci, fi, bi: (bi, 0, 0, fi, ci)),
                  pl.BlockSpec((2 * nb, 2, ft, ct), lambda ci, fi, bi: (0, 0, fi, ci))],
        out_specs=pl.BlockSpec((bt, nb + 1, 2, ft, ct), lambda ci, fi, bi: (bi, 0, 0, fi, ci)),
        out_shape=jax.ShapeDtypeStruct((b, nb + 1, 2, p, c), F32),
        compiler_params=_cparams(("parallel", "parallel", "parallel")),
        name="facc",
    )(s5, k4)
    return z.reshape(b, nb + 1, 2 * p, c)


def _hyout_kernel(zc_ref, zp_ref, inv_ref, x0_ref, s_ref, bias_ref, nw_ref, o_ref):
    y = (jnp.dot(inv_ref[0], zc_ref[0, 0].astype(BF16), preferred_element_type=F32)
         + jnp.dot(inv_ref[1], zp_ref[0, 0].astype(BF16), preferred_element_type=F32))
    y = x0_ref[0] * (y + s_ref[0] * bias_ref[...])
    ms = jnp.mean(y * y, axis=-1, keepdims=True)
    o_ref[0] = (y * lax.rsqrt(ms + NORM_EPS) * nw_ref[...]).astype(o_ref.dtype)


def _hyout(zh, inv2_bf, x0, s, hbias, hnorm_w, p):
    b, l, c = x0.shape
    nb = l // p
    tok = pl.BlockSpec((1, p, c), lambda i, a: (i, a, 0))
    return pl.pallas_call(
        _hyout_kernel,
        grid=(b, nb),
        in_specs=[pl.BlockSpec((1, 1, 2 * p, c), lambda i, a: (i, a + 1, 0, 0)),
                  pl.BlockSpec((1, 1, 2 * p, c), lambda i, a: (i, a, 0, 0)),
                  _const_spec((2, p, 2 * p)),
                  tok, tok, _const_spec((1, c)), _const_spec((1, c))],
        out_specs=tok,
        out_shape=jax.ShapeDtypeStruct((b, l, c), BF16),
        compiler_params=_cparams(("parallel", "parallel")),
        name="hyout",
    )(zh, zh, inv2_bf, x0, s, hbias.reshape(1, c), hnorm_w.reshape(1, c))


def _attn_kernel(q_ref, k_ref, v_ref, lq1_ref, lk1_ref, lq2_ref, lk2_ref, sw_ref, o_ref,
                 qs_ref, m_ref, acc_ref, *, tq, tk, nk, nch):
    lane = lax.broadcasted_iota(jnp.int32, (tq, LANES), 1)
    for c in range(nch):
        q = q_ref[0, c * tq:(c + 1) * tq, :]
        zero = jnp.zeros_like(q)
        qs_ref[c, 0:tq, :] = jnp.where(lane < DIFF_HEAD_DIM, q, zero)
        qs_ref[c, tq:2 * tq, :] = jnp.where(lane >= DIFF_HEAD_DIM, q, zero)
    m_ref[...] = jnp.full(m_ref.shape, -jnp.inf, F32)
    acc_ref[...] = jnp.zeros(acc_ref.shape, F32)
    nt = tk // LANES

    def step(i, carry):
        off = pl.multiple_of(i * tk, tk)
        kb = k_ref[0, pl.ds(off, tk), :]
        vb = v_ref[0, pl.ds(off, tk), :]
        for c in range(nch):
            s = lax.dot_general(qs_ref[c], kb, (((1,), (1,)), ((), ())), preferred_element_type=F32)
            st = [s[:, t * LANES:(t + 1) * LANES] for t in range(nt)]
            mt = st[0]
            for t in range(1, nt):
                mt = jnp.maximum(mt, st[t])
            m_prev = m_ref[c]
            m_new = jnp.maximum(m_prev, jnp.max(mt, axis=-1, keepdims=True))
            alpha = jnp.exp2(m_prev - m_new)
            pexp = jnp.concatenate([jnp.exp2(st[t] - m_new).astype(BF16) for t in range(nt)], axis=1)
            pv = jnp.dot(pexp, vb, preferred_element_type=F32)
            acc_ref[c, :, 0:LANES] = alpha * acc_ref[c, :, 0:LANES] + pv[:, 0:LANES]
            acc_ref[c, :, LANES:2 * LANES] = alpha * acc_ref[c, :, LANES:2 * LANES] + pv[:, LANES:2 * LANES]
            m_ref[c] = m_new
        return carry

    lax.fori_loop(0, nk, step, 0, unroll=8)

    lam = (jnp.exp(jnp.sum(lq1_ref[...] * lk1_ref[...], axis=-1, keepdims=True))
           - jnp.exp(jnp.sum(lq2_ref[...] * lk2_ref[...], axis=-1, keepdims=True)) + LAMBDA_INIT)
    for c in range(nch):
        o = acc_ref[c, :, 0:LANES] / acc_ref[c, :, LANES:2 * LANES]
        o = o[0:tq, :] - lam * o[tq:2 * tq, :]
        ms = jnp.mean(o * o, axis=-1, keepdims=True)
        o = (o * lax.rsqrt(ms + SUBLN_EPS) * sw_ref[...]) * (1.0 - LAMBDA_INIT)
        o_ref[0, c * tq:(c + 1) * tq, :] = o.astype(o_ref.dtype)


def _attn(q, k, v1, lq1, lk1, lq2, lk2, subln_w, tq, tk, nch):
    b, l, _ = q.shape
    nk = l // tk
    vec = lambda a: a.reshape(1, -1)
    k_spec = pl.BlockSpec((1, l, LANES), lambda i, h, j: (i, 0, h))
    v_spec = pl.BlockSpec((1, l, 2 * LANES), lambda i, h, j: (i, 0, h))
    q_spec = pl.BlockSpec((1, nch * tq, LANES), lambda i, h, j: (i, j, h))
    return pl.pallas_call(
        functools.partial(_attn_kernel, tq=tq, tk=tk, nk=nk, nch=nch),
        grid=(b, N_DIFF_HEADS, l // (nch * tq)),
        in_specs=[q_spec, k_spec, v_spec,
                  _const_spec((1, DIFF_HEAD_DIM)), _const_spec((1, DIFF_HEAD_DIM)),
                  _const_spec((1, DIFF_HEAD_DIM)), _const_spec((1, DIFF_HEAD_DIM)),
                  _const_spec((1, DIFF_V_DIM))],
        out_specs=q_spec,
        out_shape=jax.ShapeDtypeStruct((b, l, ATTN_WIDTH), BF16),
        scratch_shapes=[pltpu.VMEM((nch, 2 * tq, LANES), BF16),
                        pltpu.VMEM((nch, 2 * tq, LANES), F32),
                        pltpu.VMEM((nch, 2 * tq, 2 * LANES), F32)],
        compiler_params=_cparams(("parallel", "parallel", "parallel")),
        name="attn",
    )(q, k, v1, vec(lq1), vec(lk1), vec(lq2), vec(lk2), vec(subln_w))


def _post_kernel(x_ref, hy_ref, at_ref, mod_ref, woh_ref, woa_ref, n2_ref, w1_ref, w2_ref, fw_ref, o_ref, *, ffc):
    x = x_ref[0]
    g1 = mod_ref[0, 2:3, :]
    sh2 = mod_ref[0, 3:4, :]
    sc2 = mod_ref[0, 4:5, :]
    g2 = mod_ref[0, 5:6, :]
    mix = (jnp.dot(hy_ref[0], woh_ref[...], preferred_element_type=F32)
           + jnp.dot(at_ref[0], woa_ref[...], preferred_element_type=F32))
    x1 = x + g1 * mix
    ms = jnp.mean(x1 * x1, axis=-1, keepdims=True)
    hb = ((x1 * lax.rsqrt(ms + NORM_EPS) * n2_ref[...]) * (1.0 + sc2) + sh2).astype(BF16)
    acc = jnp.zeros_like(x1)
    for c0 in range(0, D_FF, ffc):
        hc = jnp.dot(hb, w1_ref[:, c0:c0 + ffc], preferred_element_type=F32)
        hc = jnp.square(jnp.maximum(hc, 0.0))
        acc = acc + jnp.dot(hc.astype(BF16), w2_ref[c0:c0 + ffc, :], preferred_element_type=F32)
    x2 = x1 + g2 * acc
    ms2 = jnp.mean(x2 * x2, axis=-1, keepdims=True)
    o_ref[0] = x2 * lax.rsqrt(ms2 + NORM_EPS) * fw_ref[...]


def _post(x, hy, at, mod, wo_bf, norm2_w, w1_bf, w2_bf, final_w, tm):
    b, l, d = x.shape
    tok = lambda w: pl.BlockSpec((1, tm, w), lambda i, j: (i, j, 0))
    return pl.pallas_call(
        functools.partial(_post_kernel, ffc=1024),
        grid=(b, l // tm),
        in_specs=[tok(d), tok(HYENA_WIDTH), tok(ATTN_WIDTH),
                  pl.BlockSpec((1, 6, d), lambda i, j: (i, 0, 0)),
                  _const_spec((HYENA_WIDTH, d)), _const_spec((ATTN_WIDTH, d)),
                  _const_spec((1, d)), _const_spec((d, D_FF)), _const_spec((D_FF, d)), _const_spec((1, d))],
        out_specs=tok(d),
        out_shape=jax.ShapeDtypeStruct((b, l, d), F32),
        compiler_params=_cparams(("parallel", "parallel")),
        name="post",
    )(x, hy, at, mod, wo_bf[:HYENA_WIDTH], wo_bf[HYENA_WIDTH:], norm2_w.reshape(1, d), w1_bf, w2_bf,
      final_w.reshape(1, d))


def _tiles(b, l):
    tm = min(512, l)
    p = min(512, l // 2)
    nch = 4
    tq = min(256, l // nch)
    tk = min(512, l)
    nb = l // p
    bt = math.gcd(b, 8)
    per_ft = LANES * 4 * 2 * (bt * nb + 2 * nb + bt * (nb + 1))
    ft = SUBLANES
    while ft * 2 <= min(p, 64) and per_ft * ft * 2 <= 8 * 1024 * 1024:
        ft *= 2
    return dict(tm=tm, p=p, tq=tq, tk=tk, nch=nch, bt=bt, ft=ft)


def _encoder(x, mod, w, tables):
    b, l, d = x.shape
    t = _tiles(b, l)
    p = t["p"]
    fwd2_bf, inv2_bf = tables["fwd2"], tables["inv2"]
    cos_t, sa_t, sb_t = _rope_tables(l)
    u, q, k, v = _inproj(x, mod, w["norm1_w"], w["w_in"], cos_t, sa_t, sb_t, t["tm"])
    kh = _filter_blocks(l, p, w["filt_w1"], w["filt_b1"], w["filt_w2"], w["filt_b2"], w["filt_w3"], w["filt_b3"],
                        w["filt_w4"], w["filt_freq"], fwd2_bf)
    sh, x0, s = _sigdft(u, w["conv_w"], w["conv_b"], fwd2_bf, p)
    zh = _facc(sh, kh, p, t["bt"], t["ft"])
    hy = _hyout(zh, inv2_bf, x0, s, w["hyena_bias"], w["hyena_norm_w"], p)
    at = _attn(q, k, v, w["lambda_q1"], w["lambda_k1"], w["lambda_q2"], w["lambda_k2"], w["subln_w"],
               t["tq"], t["tk"], t["nch"])
    return _post(x, hy, at, mod, w["w_out"], w["norm2_w"], w["w_mlp1"], w["w_mlp2"], w["final_w"], t["tm"])


def kernel(x_prompt, x_sample, c_prompt, c_sample, w_ada, b_ada, norm1_w, w_in, conv_w, conv_b, filt_w1, filt_b1, filt_w2, filt_b2, filt_w3, filt_b3, filt_w4, filt_freq, hyena_bias, hyena_norm_w, lambda_q1, lambda_k1, lambda_q2, lambda_k2, subln_w, w_out, norm2_w, w_mlp1, w_mlp2, final_w):
    assert w_ada.shape[0] == 1, "single-layer encoder"
    w = dict(norm1_w=norm1_w[0], w_in=w_in[0].astype(BF16), conv_w=conv_w[0], conv_b=conv_b[0],
             filt_w1=filt_w1[0], filt_b1=filt_b1[0], filt_w2=filt_w2[0], filt_b2=filt_b2[0],
             filt_w3=filt_w3[0], filt_b3=filt_b3[0], filt_w4=filt_w4[0], filt_freq=filt_freq[0],
             hyena_bias=hyena_bias[0], hyena_norm_w=hyena_norm_w[0], lambda_q1=lambda_q1[0],
             lambda_k1=lambda_k1[0], lambda_q2=lambda_q2[0], lambda_k2=lambda_k2[0], subln_w=subln_w[0],
             w_out=w_out[0].astype(BF16), norm2_w=norm2_w[0], w_mlp1=w_mlp1[0].astype(BF16),
             w_mlp2=w_mlp2[0].astype(BF16), final_w=final_w)
    nbp = c_prompt.shape[0]
    mod = _ada(jnp.concatenate([c_prompt, c_sample], axis=0), w_ada[0], b_ada[0])
    mod = mod.reshape(mod.shape[0], 6, D_MODEL)
    outs = []
    for x, m in ((x_prompt, mod[:nbp]), (x_sample, mod[nbp:])):
        p = _tiles(*x.shape[:2])["p"]
        fwd2, inv2 = _dft_mats(p)
        tables = dict(fwd2=fwd2.astype(BF16), inv2=inv2.astype(BF16))
        outs.append(_encoder(x, m, w, tables))
    return tuple(outs)
```

```python
import functools
import math

import numpy as np
import jax
import jax.numpy as jnp
from jax import lax
from jax.experimental import pallas as pl
from jax.experimental.pallas import tpu as pltpu

F32 = jnp.float32
BF16 = jnp.bfloat16

D_MODEL = 1024
HYENA_WIDTH = 512
ATTN_WIDTH = 512
N_DIFF_HEADS = 4
DIFF_HEAD_DIM = 64
DIFF_V_DIM = 2 * DIFF_HEAD_DIM
ROT_DIM = DIFF_HEAD_DIM // 4
ROPE_THETA = 500000.0
D_IN = 3 * HYENA_WIDTH + 3 * ATTN_WIDTH
D_FF = 4 * D_MODEL
FILTER_EMB = 33
FILTER_ORDER = 64
FAST_DECAY_PCT = 0.3
SLOW_DECAY_PCT = 1.5
DECAY_TARGET = 1e-2
NORM_EPS = 1e-6
SUBLN_EPS = 1e-5
LAMBDA_INIT = 0.8 - 0.6 * math.exp(-0.3 * 0)

LANES = 128
SUBLANES = 8
HALO = 2 * SUBLANES
VMEM_LIMIT = 56 * 1024 * 1024

HIGHEST = lax.Precision.HIGHEST


def _cparams(sem):
    return pltpu.CompilerParams(dimension_semantics=sem, vmem_limit_bytes=VMEM_LIMIT)


def _const_spec(shape):
    nd = len(shape)
    return pl.BlockSpec(shape, lambda *_: (0,) * nd, pipeline_mode=pl.Buffered(1))


def _ada_kernel(c_ref, w_ref, b_ref, o_ref):
    c = c_ref[...]
    s = c * jax.nn.sigmoid(c)
    o_ref[...] = jnp.dot(s.astype(BF16), w_ref[...].astype(BF16), preferred_element_type=F32) + b_ref[...]


def _ada(c, w_ada, b_ada):
    nb_, d = c.shape
    n = w_ada.shape[1]
    tn = 1024
    return pl.pallas_call(
        _ada_kernel,
        grid=(n // tn,),
        in_specs=[pl.BlockSpec((nb_, d), lambda j: (0, 0)),
                  pl.BlockSpec((d, tn), lambda j: (0, j)),
                  pl.BlockSpec((1, tn), lambda j: (0, j))],
        out_specs=pl.BlockSpec((nb_, tn), lambda j: (0, j)),
        out_shape=jax.ShapeDtypeStruct((nb_, n), F32),
        compiler_params=_cparams(("parallel",)),
        name="ada",
    )(c, w_ada, b_ada.reshape(1, n))


def _inproj_kernel(x_ref, mod_ref, nw_ref, w_ref, cos_ref, sin_ref, u_ref, q_ref, k_ref, v_ref):
    x = x_ref[0]
    sh = mod_ref[0, 0:1, :]
    sc = mod_ref[0, 1:2, :]
    ms = jnp.mean(x * x, axis=-1, keepdims=True)
    h = (x * lax.rsqrt(ms + NORM_EPS) * nw_ref[...]) * (1.0 + sc) + sh
    hb = h.astype(BF16)
    hw3 = 3 * HYENA_WIDTH
    for c0 in range(0, hw3, 512):
        u_ref[0, :, c0:c0 + 512] = jnp.dot(hb, w_ref[:, c0:c0 + 512], preferred_element_type=F32).astype(BF16)
    cos_t = cos_ref[...]
    sin_t = sin_ref[...]
    hl = lax.broadcasted_iota(jnp.int32, sin_t.shape, 1) % DIFF_HEAD_DIM
    sa = jnp.where(hl < ROT_DIM // 2, -sin_t, 0.0)
    sb = jnp.where(jnp.logical_and(hl >= ROT_DIM // 2, hl < ROT_DIM), sin_t, 0.0)

    def rope(xg):
        return xg * cos_t + pltpu.roll(xg, LANES - ROT_DIM // 2, axis=1) * sa + pltpu.roll(xg, ROT_DIM // 2, axis=1) * sb

    pq = jnp.dot(hb, w_ref[:, hw3:hw3 + ATTN_WIDTH], preferred_element_type=F32)
    pk = jnp.dot(hb, w_ref[:, hw3 + ATTN_WIDTH:hw3 + 2 * ATTN_WIDTH], preferred_element_type=F32)
    for g in range(ATTN_WIDTH // LANES):
        sl = slice(g * LANES, (g + 1) * LANES)
        q_ref[0, :, sl] = (rope(pq[:, sl]) * (math.log2(math.e) * DIFF_HEAD_DIM ** -0.5)).astype(BF16)
        k_ref[0, :, sl] = rope(pk[:, sl]).astype(BF16)
    pv = jnp.dot(hb, w_ref[:, hw3 + 2 * ATTN_WIDTH:], preferred_element_type=F32).astype(BF16)
    ones = jnp.ones((pv.shape[0], DIFF_V_DIM), BF16)
    for g in range(N_DIFF_HEADS):
        v_ref[0, :, 2 * g * DIFF_V_DIM:(2 * g + 1) * DIFF_V_DIM] = pv[:, g * DIFF_V_DIM:(g + 1) * DIFF_V_DIM]
        v_ref[0, :, (2 * g + 1) * DIFF_V_DIM:(2 * g + 2) * DIFF_V_DIM] = ones


def _inproj(x, mod, norm_w, w_in_bf, cos_t, sin_t, tm):
    b, l, d = x.shape
    hw3 = 3 * HYENA_WIDTH
    tok = lambda w: pl.BlockSpec((1, tm, w), lambda i, j: (i, j, 0))
    tab = pl.BlockSpec((tm, LANES), lambda i, j: (j, 0))
    return pl.pallas_call(
        _inproj_kernel,
        grid=(b, l // tm),
        in_specs=[tok(d),
                  pl.BlockSpec((1, 6, d), lambda i, j: (i, 0, 0)),
                  _const_spec((1, d)),
                  _const_spec((d, D_IN)),
                  tab, tab],
        out_specs=[tok(hw3), tok(ATTN_WIDTH), tok(ATTN_WIDTH), tok(2 * ATTN_WIDTH)],
        out_shape=[jax.ShapeDtypeStruct((b, l, hw3), BF16),
                   jax.ShapeDtypeStruct((b, l, ATTN_WIDTH), BF16),
                   jax.ShapeDtypeStruct((b, l, ATTN_WIDTH), BF16),
                   jax.ShapeDtypeStruct((b, l, 2 * ATTN_WIDTH), BF16)],
        compiler_params=_cparams(("parallel", "parallel")),
        name="inproj",
    )(x, mod, norm_w.reshape(1, d), w_in_bf, cos_t, sin_t)


def _rope_tables(l):
    inv_freq = ROPE_THETA ** (-jnp.arange(0, ROT_DIM, 2, dtype=F32) / ROT_DIM)
    hl = np.arange(LANES) % DIFF_HEAD_DIM
    lane_freq = jnp.where(hl < ROT_DIM, inv_freq[hl % (ROT_DIM // 2)], 0.0)
    ang = jnp.arange(l, dtype=F32)[:, None] * lane_freq[None, :]
    return jnp.cos(ang), jnp.sin(ang)


def _dft_mats(p):
    f = np.arange(p, dtype=np.float64)
    n = np.arange(p, dtype=np.float64)
    th = np.pi * (2.0 * f[:, None] + 1.0) * n[None, :] / (2.0 * p)
    fwd = np.concatenate([np.cos(th), -np.sin(th)], axis=0)
    n2 = np.arange(2 * p, dtype=np.float64)
    th2 = np.pi * (2.0 * f[None, :] + 1.0) * n2[:, None] / (2.0 * p)
    inv = np.concatenate([np.cos(th2), -np.sin(th2)], axis=1) / p
    fwd2 = np.stack([fwd, fwd[:, ::-1]], axis=0).astype(np.float32)
    inv2 = np.stack([inv[:p], inv[p:]], axis=0).astype(np.float32)
    return jnp.asarray(fwd2), jnp.asarray(inv2)


def _filt_kernel(fb_ref, w1t_ref, w1c_ref, w1s_ref, b1_ref, w2_ref, b2_ref, w3_ref, b3_ref, w4h_ref, w4b_ref,
                 fr_ref, ad_ref, mat_ref, o_ref, *, l, p, nb):
    j = pl.program_id(0)
    neg = j < nb
    e = jnp.where(neg, nb - 1 - j, j - nb)
    base = e * p + jnp.where(neg, 1, 0)
    posl = (base + lax.broadcasted_iota(jnp.int32, (1, p), 1)).astype(F32)
    tl = posl / float(l - 1)
    wl = (2.0 * math.pi) * posl / float(l)
    ang = fb_ref[...] * wl
    fr = fr_ref[...]
    dot = functools.partial(jnp.dot, precision=HIGHEST, preferred_element_type=F32)
    z1 = w1t_ref[...] * tl + dot(w1c_ref[...], jnp.cos(ang)) - dot(w1s_ref[...], jnp.sin(ang))
    h = jnp.sin(fr * (z1 + b1_ref[...]))
    h = jnp.sin(fr * (dot(w2_ref[...], h) + b2_ref[...]))
    h = jnp.sin(fr * (dot(w3_ref[...], h) + b3_ref[...]))
    row = lax.broadcasted_iota(jnp.int32, (p, 1), 0)
    pos = base + row
    t = pos.astype(F32) / float(l - 1)
    kap = lax.dot_general(h, w4h_ref[...], (((0,), (0,)), ((), ())), precision=HIGHEST,
                          preferred_element_type=F32) * jnp.exp(-t * ad_ref[...])
    kap = jnp.where(pos < l, kap, 0.0)
    hb0 = jnp.sum(h[:, 0:1] * w4b_ref[...], axis=0, keepdims=True)
    kap = kap + jnp.where(jnp.logical_and(row == 0, j == nb), hb0, 0.0)
    o_ref[0] = jnp.dot(mat_ref[0], kap.astype(BF16), preferred_element_type=F32)


def _filter_blocks(l, p, fw1, fb1, fw2, fb2, fw3, fb3, fw4, ffreq, fwd2_bf):
    nb = l // p
    c = HYENA_WIDTH
    bands = (FILTER_EMB - 1) // 2
    fbands = jnp.linspace(1e-4, bands - 1, bands, dtype=F32)[:, None]
    col = lambda a: a.reshape(-1, 1)
    min_decay = math.log(DECAY_TARGET) / SLOW_DECAY_PCT
    max_decay = math.log(DECAY_TARGET) / FAST_DECAY_PCT
    absd = jnp.abs(jnp.linspace(min_decay, max_decay, c, dtype=F32))[None, :]
    half = lambda j: jnp.where(j < nb, 1, 0)
    args = (fbands, col(fw1[0]), fw1[1:1 + bands].T, fw1[1 + bands:].T, col(fb1), fw2.T, col(fb2),
            fw3.T, col(fb3), fw4, fw4[:, c:], col(ffreq), absd, fwd2_bf)
    specs = [_const_spec(a.shape) for a in args]
    specs[9] = pl.BlockSpec((FILTER_ORDER, c), lambda j: (0, half(j)))
    specs[13] = pl.BlockSpec((1, 2 * p, p), lambda j: (half(j), 0, 0))
    return pl.pallas_call(
        functools.partial(_filt_kernel, l=l, p=p, nb=nb),
        grid=(2 * nb,),
        in_specs=specs,
        out_specs=pl.BlockSpec((1, 2 * p, c), lambda j: (j, 0, 0)),
        out_shape=jax.ShapeDtypeStruct((2 * nb, 2 * p, c), F32),
        compiler_params=_cparams(("parallel",)),
        name="filt",
    )(*args)


def _sigdft_kernel(u_ref, prev_ref, next_ref, cw_ref, cb_ref, fwd_ref, sh_ref, x0_ref, s_ref, *, p, nb):
    a = pl.program_id(1)
    c = HYENA_WIDTH
    u = u_ref[0].astype(F32)
    row = lax.broadcasted_iota(jnp.int32, (p, 1), 0)
    prev_row = jnp.where(a > 0, prev_ref[0, HALO - 1:HALO, :].astype(F32), 0.0)
    next_row = jnp.where(a < nb - 1, next_ref[0, 0:1, :].astype(F32), 0.0)
    up = jnp.where(row == 0, prev_row, pltpu.roll(u, 1, axis=0))
    un = jnp.where(row == p - 1, next_row, pltpu.roll(u, p - 1, axis=0))
    cv = up * cw_ref[0:1, :] + u * cw_ref[1:2, :] + un * cw_ref[2:3, :] + cb_ref[...]
    s = (cv[:, c:2 * c] * cv[:, 2 * c:]).astype(BF16)
    x0_ref[0] = cv[:, :c].astype(BF16)
    s_ref[0] = s
    sh_ref[0, 0] = jnp.dot(fwd_ref[0], s, preferred_element_type=F32)


def _sigdft(u, conv_w, conv_b, fwd2_bf, p):
    b, l, c3 = u.shape
    nb = l // p
    c = HYENA_WIDTH
    rh = p // HALO
    nrowh = l // HALO
    return pl.pallas_call(
        functools.partial(_sigdft_kernel, p=p, nb=nb),
        grid=(b, nb),
        in_specs=[pl.BlockSpec((1, p, c3), lambda i, a: (i, a, 0)),
                  pl.BlockSpec((1, HALO, c3), lambda i, a: (i, jnp.maximum(a * rh - 1, 0), 0)),
                  pl.BlockSpec((1, HALO, c3), lambda i, a: (i, jnp.minimum((a + 1) * rh, nrowh - 1), 0)),
                  _const_spec((3, c3)), _const_spec((1, c3)),
                  pl.BlockSpec((1, 2 * p, p), lambda i, a: (0, 0, 0), pipeline_mode=pl.Buffered(1))],
        out_specs=[pl.BlockSpec((1, 1, 2 * p, c), lambda i, a: (i, a, 0, 0)),
                   pl.BlockSpec((1, p, c), lambda i, a: (i, a, 0)),
                   pl.BlockSpec((1, p, c), lambda i, a: (i, a, 0))],
        out_shape=[jax.ShapeDtypeStruct((b, nb, 2 * p, c), F32),
                   jax.ShapeDtypeStruct((b, l, c), BF16),
                   jax.ShapeDtypeStruct((b, l, c), BF16)],
        compiler_params=_cparams(("parallel", "parallel")),
        name="sigdft",
    )(u, u, u, conv_w, conv_b.reshape(1, c3), fwd2_bf)


def _facc_kernel(s_ref, k_ref, z_ref, ss_ref, ks_ref, *, nb, rt):
    bt, _, _, ft, ct = s_ref.shape
    ss_ref[...] = s_ref[:, :, 0] + s_ref[:, :, 1]
    ks_ref[...] = k_ref[:, 0] + k_ref[:, 1]

    def per_tile(ti, carry0):
        rows = pl.ds(pl.multiple_of(ti * rt, rt), rt)

        def per_e(e, carry):
            def per_a(a, acc):
                t1, t2, t3 = acc
                j = e - 1 - a + nb
                t1 = t1 + k_ref[j, 0, rows, :][None] * s_ref[:, a, 0, rows, :]
                t2 = t2 + k_ref[j, 1, rows, :][None] * s_ref[:, a, 1, rows, :]
                t3 = t3 + ks_ref[j, rows, :][None] * ss_ref[:, a, rows, :]
                return (t1, t2, t3)

            zero = jnp.zeros((bt, rt, ct), F32)
            t1, t2, t3 = lax.fori_loop(0, nb, per_a, (zero, zero, zero), unroll=min(nb, 4))
            z_ref[:, e, 0, rows, :] = t1 - t2
            z_ref[:, e, 1, rows, :] = t3 - t1 - t2
            return carry

        lax.fori_loop(0, nb + 1, per_e, 0)
        return carry0

    lax.fori_loop(0, ft // rt, per_tile, 0)


def _facc(sh, kh, p, bt, ft):
    b, nb = sh.shape[0], sh.shape[1]
    c = HYENA_WIDTH
    ct = LANES
    s5 = sh.reshape(b, nb, 2, p, c)
    k4 = kh.reshape(2 * nb, 2, p, c)
    rt = min(ft, (8 * SUBLANES) // bt)
    z = pl.pallas_call(
        functools.partial(_facc_kernel, nb=nb, rt=rt),
        grid=(c // ct, p // ft, b // bt),
        in_specs=[pl.BlockSpec((bt, nb, 2, ft, ct), lambda ci, fi, bi: (bi, 0, 0, fi, ci)),
                  pl.BlockSpec((2 * nb, 2, ft, ct), lambda ci, fi, bi: (0, 0, fi, ci))],
        out_specs=pl.BlockSpec((bt, nb + 1, 2, ft, ct), lambda ci, fi, bi: (bi, 0, 0, fi, ci)),
        out_shape=jax.ShapeDtypeStruct((b, nb + 1, 2, p, c), F32),
        scratch_shapes=[pltpu.VMEM((bt, nb, ft, ct), F32), pltpu.VMEM((2 * nb, ft, ct), F32)],
        compiler_params=_cparams(("parallel", "parallel", "parallel")),
        name="facc",
    )(s5, k4)
    return z.reshape(b, nb + 1, 2 * p, c)


def _hyout_kernel(z_ref, inv_ref, x0_ref, s_ref, bias_ref, nw_ref, o_ref, tail_ref):
    g = pl.program_id(1)
    zb = z_ref[0, 0].astype(BF16)

    @pl.when(g > 0)
    def _():
        y = tail_ref[...] + jnp.dot(inv_ref[0], zb, preferred_element_type=F32)
        y = x0_ref[0].astype(F32) * (y + s_ref[0].astype(F32) * bias_ref[...])
        ms = jnp.mean(y * y, axis=-1, keepdims=True)
        o_ref[0] = (y * lax.rsqrt(ms + NORM_EPS) * nw_ref[...]).astype(o_ref.dtype)

    tail_ref[...] = jnp.dot(inv_ref[1], zb, preferred_element_type=F32)


def _hyout(zh, inv2_bf, x0, s, hbias, hnorm_w, p):
    b, l, c = x0.shape
    nb = l // p
    tok = pl.BlockSpec((1, p, c), lambda i, g: (i, jnp.maximum(g - 1, 0), 0))
    return pl.pallas_call(
        _hyout_kernel,
        grid=(b, nb + 1),
        in_specs=[pl.BlockSpec((1, 1, 2 * p, c), lambda i, g: (i, g, 0, 0)),
                  _const_spec((2, p, 2 * p)),
                  tok, tok, _const_spec((1, c)), _const_spec((1, c))],
        out_specs=tok,
        out_shape=jax.ShapeDtypeStruct((b, l, c), BF16),
        scratch_shapes=[pltpu.VMEM((p, c), F32)],
        compiler_params=_cparams(("parallel", "arbitrary")),
        name="hyout",
    )(zh, inv2_bf, x0, s, hbias.reshape(1, c), hnorm_w.reshape(1, c))


def _attn_kernel(q_ref, k_ref, v_ref, lq1_ref, lk1_ref, lq2_ref, lk2_ref, sw_ref, o_ref,
                 qs_ref, m_ref, acc_ref, *, tq, tk, nk, nch):
    lane = lax.broadcasted_iota(jnp.int32, (tq, LANES), 1)
    for c in range(nch):
        q = q_ref[0, c * tq:(c + 1) * tq, :]
        zero = jnp.zeros_like(q)
        qs_ref[c, 0:tq, :] = jnp.where(lane < DIFF_HEAD_DIM, q, zero)
        qs_ref[c, tq:2 * tq, :] = jnp.where(lane >= DIFF_HEAD_DIM, q, zero)
    m_ref[...] = jnp.full(m_ref.shape, -jnp.inf, F32)
    acc_ref[...] = jnp.zeros(acc_ref.shape, F32)
    nt = tk // LANES

    def step(i, carry):
        off = pl.multiple_of(i * tk, tk)
        kb = k_ref[0, pl.ds(off, tk), :]
        vb = v_ref[0, pl.ds(off, tk), :]
        for c in range(nch):
            s = lax.dot_general(qs_ref[c], kb, (((1,), (1,)), ((), ())), preferred_element_type=F32)
            st = [s[:, t * LANES:(t + 1) * LANES] for t in range(nt)]
            mt = st[0]
            for t in range(1, nt):
                mt = jnp.maximum(mt, st[t])
            m_prev = m_ref[c]
            m_new = jnp.maximum(m_prev, jnp.max(mt, axis=-1, keepdims=True))
            alpha = jnp.exp2(m_prev - m_new)
            pexp = jnp.concatenate([jnp.exp2(st[t] - m_new).astype(BF16) for t in range(nt)], axis=1)
            pv = jnp.dot(pexp, vb, preferred_element_type=F32)
            acc_ref[c, :, 0:LANES] = alpha * acc_ref[c, :, 0:LANES] + pv[:, 0:LANES]
            acc_ref[c, :, LANES:2 * LANES] = alpha * acc_ref[c, :, LANES:2 * LANES] + pv[:, LANES:2 * LANES]
            m_ref[c] = m_new
        return carry

    lax.fori_loop(0, nk, step, 0, unroll=8)

    lam = (jnp.exp(jnp.sum(lq1_ref[...] * lk1_ref[...], axis=-1, keepdims=True))
           - jnp.exp(jnp.sum(lq2_ref[...] * lk2_ref[...], axis=-1, keepdims=True)) + LAMBDA_INIT)
    for c in range(nch):
        o = acc_ref[c, :, 0:LANES] / acc_ref[c, :, LANES:2 * LANES]
        o = o[0:tq, :] - lam * o[tq:2 * tq, :]
        ms = jnp.mean(o * o, axis=-1, keepdims=True)
        o = (o * lax.rsqrt(ms + SUBLN_EPS) * sw_ref[...]) * (1.0 - LAMBDA_INIT)
        o_ref[0, c * tq:(c + 1) * tq, :] = o.astype(o_ref.dtype)


def _attn(q, k, v1, lq1, lk1, lq2, lk2, subln_w, tq, tk, nch):
    b, l, _ = q.shape
    nk = l // tk
    vec = lambda a: a.reshape(1, -1)
    k_spec = pl.BlockSpec((1, l, LANES), lambda i, h, j: (i, 0, h))
    v_spec = pl.BlockSpec((1, l, 2 * LANES), lambda i, h, j: (i, 0, h))
    q_spec = pl.BlockSpec((1, nch * tq, LANES), lambda i, h, j: (i, j, h))
    return pl.pallas_call(
        functools.partial(_attn_kernel, tq=tq, tk=tk, nk=nk, nch=nch),
        grid=(b, N_DIFF_HEADS, l // (nch * tq)),
        in_specs=[q_spec, k_spec, v_spec,
                  _const_spec((1, DIFF_HEAD_DIM)), _const_spec((1, DIFF_HEAD_DIM)),
                  _const_spec((1, DIFF_HEAD_DIM)), _const_spec((1, DIFF_HEAD_DIM)),
                  _const_spec((1, DIFF_V_DIM))],
        out_specs=q_spec,
        out_shape=jax.ShapeDtypeStruct((b, l, ATTN_WIDTH), BF16),
        scratch_shapes=[pltpu.VMEM((nch, 2 * tq, LANES), BF16),
                        pltpu.VMEM((nch, 2 * tq, LANES), F32),
                        pltpu.VMEM((nch, 2 * tq, 2 * LANES), F32)],
        compiler_params=_cparams(("parallel", "parallel", "parallel")),
        name="attn",
    )(q, k, v1, vec(lq1), vec(lk1), vec(lq2), vec(lk2), vec(subln_w))


def _post_kernel(x_ref, hy_ref, at_ref, mod_ref, woh_ref, woa_ref, n2_ref, w1_ref, w2_ref, fw_ref, o_ref, *, ffc):
    x = x_ref[0]
    g1 = mod_ref[0, 2:3, :]
    sh2 = mod_ref[0, 3:4, :]
    sc2 = mod_ref[0, 4:5, :]
    g2 = mod_ref[0, 5:6, :]
    mix = (jnp.dot(hy_ref[0], woh_ref[...], preferred_element_type=F32)
           + jnp.dot(at_ref[0], woa_ref[...], preferred_element_type=F32))
    x1 = x + g1 * mix
    ms = jnp.mean(x1 * x1, axis=-1, keepdims=True)
    hb = ((x1 * lax.rsqrt(ms + NORM_EPS) * n2_ref[...]) * (1.0 + sc2) + sh2).astype(BF16)
    acc = jnp.zeros_like(x1)
    for c0 in range(0, D_FF, ffc):
        hc = jnp.dot(hb, w1_ref[:, c0:c0 + ffc], preferred_element_type=F32)
        hc = jnp.square(jnp.maximum(hc, 0.0))
        acc = acc + jnp.dot(hc.astype(BF16), w2_ref[c0:c0 + ffc, :], preferred_element_type=F32)
    x2 = x1 + g2 * acc
    ms2 = jnp.mean(x2 * x2, axis=-1, keepdims=True)
    o_ref[0] = x2 * lax.rsqrt(ms2 + NORM_EPS) * fw_ref[...]


def _post(x, hy, at, mod, wo_bf, norm2_w, w1_bf, w2_bf, final_w, tm):
    b, l, d = x.shape
    tok = lambda w: pl.BlockSpec((1, tm, w), lambda i, j: (i, j, 0))
    return pl.pallas_call(
        functools.partial(_post_kernel, ffc=1024),
        grid=(b, l // tm),
        in_specs=[tok(d), tok(HYENA_WIDTH), tok(ATTN_WIDTH),
                  pl.BlockSpec((1, 6, d), lambda i, j: (i, 0, 0)),
                  _const_spec((HYENA_WIDTH, d)), _const_spec((ATTN_WIDTH, d)),
                  _const_spec((1, d)), _const_spec((d, D_FF)), _const_spec((D_FF, d)), _const_spec((1, d))],
        out_specs=tok(d),
        out_shape=jax.ShapeDtypeStruct((b, l, d), F32),
        compiler_params=_cparams(("parallel", "parallel")),
        name="post",
    )(x, hy, at, mod, wo_bf[:HYENA_WIDTH], wo_bf[HYENA_WIDTH:], norm2_w.reshape(1, d), w1_bf, w2_bf,
      final_w.reshape(1, d))


def _tiles(b, l):
    tm = min(512, l)
    p = 1024 if l >= 8192 else min(512, l // 2)
    nch = 4
    tq = min(256, l // nch)
    tk = min(512, l)
    nb = l // p
    bt = math.gcd(b, 2)
    per_ft = LANES * 4 * 2 * (bt * nb + 2 * nb + bt * (nb + 1))
    ft = SUBLANES
    while ft * 2 <= min(p, 128) and per_ft * ft * 2 <= 8 * 1024 * 1024:
        ft *= 2
    return dict(tm=tm, p=p, tq=tq, tk=tk, nch=nch, bt=bt, ft=ft)


def _encoder(x, mod, w, tables):
    b, l, d = x.shape
    t = _tiles(b, l)
    p = t["p"]
    fwd2_bf, inv2_bf = tables["fwd2"], tables["inv2"]
    cos_t, sin_t = _rope_tables(l)
    u, q, k, v = _inproj(x, mod, w["norm1_w"], w["w_in"], cos_t, sin_t, t["tm"])
    kh = _filter_blocks(l, p, w["filt_w1"], w["filt_b1"], w["filt_w2"], w["filt_b2"], w["filt_w3"], w["filt_b3"],
                        w["filt_w4"], w["filt_freq"], fwd2_bf)
    sh, x0, s = _sigdft(u, w["conv_w"], w["conv_b"], fwd2_bf, p)
    zh = _facc(sh, kh, p, t["bt"], t["ft"])
    hy = _hyout(zh, inv2_bf, x0, s, w["hyena_bias"], w["hyena_norm_w"], p)
    at = _attn(q, k, v, w["lambda_q1"], w["lambda_k1"], w["lambda_q2"], w["lambda_k2"], w["subln_w"],
               t["tq"], t["tk"], t["nch"])
    return _post(x, hy, at, mod, w["w_out"], w["norm2_w"], w["w_mlp1"], w["w_mlp2"], w["final_w"], t["tm"])


def kernel(x_prompt, x_sample, c_prompt, c_sample, w_ada, b_ada, norm1_w, w_in, conv_w, conv_b, filt_w1, filt_b1, filt_w2, filt_b2, filt_w3, filt_b3, filt_w4, filt_freq, hyena_bias, hyena_norm_w, lambda_q1, lambda_k1, lambda_q2, lambda_k2, subln_w, w_out, norm2_w, w_mlp1, w_mlp2, final_w):
    assert w_ada.shape[0] == 1, "single-layer encoder"
    w = dict(norm1_w=norm1_w[0], w_in=w_in[0].astype(BF16), conv_w=conv_w[0], conv_b=conv_b[0],
             filt_w1=filt_w1[0], filt_b1=filt_b1[0], filt_w2=filt_w2[0], filt_b2=filt_b2[0],
             filt_w3=filt_w3[0], filt_b3=filt_b3[0], filt_w4=filt_w4[0], filt_freq=filt_freq[0],
             hyena_bias=hyena_bias[0], hyena_norm_w=hyena_norm_w[0], lambda_q1=lambda_q1[0],
             lambda_k1=lambda_k1[0], lambda_q2=lambda_q2[0], lambda_k2=lambda_k2[0], subln_w=subln_w[0],
             w_out=w_out[0].astype(BF16), norm2_w=norm2_w[0], w_mlp1=w_mlp1[0].astype(BF16),
             w_mlp2=w_mlp2[0].astype(BF16), final_w=final_w)
    nbp = c_prompt.shape[0]
    mod = _ada(jnp.concatenate([c_prompt, c_sample], axis=0), w_ada[0], b_ada[0])
    mod = mod.reshape(mod.shape[0], 6, D_MODEL)
    outs = []
    for x, m in ((x_prompt, mod[:nbp]), (x_sample, mod[nbp:])):
        p = _tiles(*x.shape[:2])["p"]
        fwd2, inv2 = _dft_mats(p)
        tables = dict(fwd2=fwd2.astype(BF16), inv2=inv2.astype(BF16))
        outs.append(_encoder(x, m, w, tables))
    return tuple(outs)
```

```python
import functools
import math

import numpy as np
import jax
import jax.numpy as jnp
from jax import lax
from jax.experimental import pallas as pl
from jax.experimental.pallas import tpu as pltpu

F32 = jnp.float32
BF16 = jnp.bfloat16

D_MODEL = 1024
HYENA_WIDTH = 512
ATTN_WIDTH = 512
N_DIFF_HEADS = 4
DIFF_HEAD_DIM = 64
DIFF_V_DIM = 2 * DIFF_HEAD_DIM
ROT_DIM = DIFF_HEAD_DIM // 4
ROPE_THETA = 500000.0
D_IN = 3 * HYENA_WIDTH + 3 * ATTN_WIDTH
D_FF = 4 * D_MODEL
FILTER_EMB = 33
FILTER_ORDER = 64
FAST_DECAY_PCT = 0.3
SLOW_DECAY_PCT = 1.5
DECAY_TARGET = 1e-2
NORM_EPS = 1e-6
SUBLN_EPS = 1e-5
LAMBDA_INIT = 0.8 - 0.6 * math.exp(-0.3 * 0)

LANES = 128
SUBLANES = 8
HALO = 2 * SUBLANES
VMEM_LIMIT = 56 * 1024 * 1024

HIGHEST = lax.Precision.HIGHEST


def _cparams(sem):
    return pltpu.CompilerParams(dimension_semantics=sem, vmem_limit_bytes=VMEM_LIMIT)


def _const_spec(shape):
    nd = len(shape)
    return pl.BlockSpec(shape, lambda *_: (0,) * nd, pipeline_mode=pl.Buffered(1))


def _ada_kernel(c_ref, w_ref, b_ref, o_ref):
    c = c_ref[...]
    s = c * jax.nn.sigmoid(c)
    o_ref[...] = jnp.dot(s.astype(BF16), w_ref[...].astype(BF16), preferred_element_type=F32) + b_ref[...]


def _ada(c, w_ada, b_ada):
    nb_, d = c.shape
    n = w_ada.shape[1]
    tn = 1024
    return pl.pallas_call(
        _ada_kernel,
        grid=(n // tn,),
        in_specs=[pl.BlockSpec((nb_, d), lambda j: (0, 0)),
                  pl.BlockSpec((d, tn), lambda j: (0, j)),
                  pl.BlockSpec((1, tn), lambda j: (0, j))],
        out_specs=pl.BlockSpec((nb_, tn), lambda j: (0, j)),
        out_shape=jax.ShapeDtypeStruct((nb_, n), F32),
        compiler_params=_cparams(("parallel",)),
        name="ada",
    )(c, w_ada, b_ada.reshape(1, n))


def _inproj_kernel(x_ref, xp_ref, xn_ref, mod_ref, nw_ref, w_ref, cos_ref, sin_ref, cw_ref, cb_ref, fwd_ref,
                   sh_ref, x0_ref, s_ref, q_ref, k_ref, v_ref, *, nt, r):
    j = pl.program_id(1)
    tm = x_ref.shape[1]
    c = HYENA_WIDTH
    hw3 = 3 * c
    shift = mod_ref[0, 0:1, :]
    scale = mod_ref[0, 1:2, :]

    def normmod(x):
        ms = jnp.mean(x * x, axis=-1, keepdims=True)
        return ((x * lax.rsqrt(ms + NORM_EPS) * nw_ref[...]) * (1.0 + scale) + shift).astype(BF16)

    hb = normmod(x_ref[0])
    hp = jnp.where(j > 0, normmod(xp_ref[0]), jnp.zeros((HALO, hb.shape[1]), BF16))
    hn = jnp.where(j < nt - 1, normmod(xn_ref[0]), jnp.zeros((HALO, hb.shape[1]), BF16))
    he = jnp.concatenate([hp, hb, hn], axis=0)
    cvs = []
    for c0 in range(0, hw3, c):
        ue = jnp.dot(he, w_ref[:, c0:c0 + c], preferred_element_type=F32)
        cvs.append(ue[HALO - 1:HALO - 1 + tm] * cw_ref[0:1, c0:c0 + c] + ue[HALO:HALO + tm] * cw_ref[1:2, c0:c0 + c]
                   + ue[HALO + 1:HALO + 1 + tm] * cw_ref[2:3, c0:c0 + c] + cb_ref[:, c0:c0 + c])
    s = (cvs[1] * cvs[2]).astype(BF16)
    x0_ref[0] = cvs[0].astype(BF16)
    s_ref[0] = s
    part = jnp.dot(fwd_ref[...], s, preferred_element_type=F32)
    if r == 1:
        sh_ref[0, 0] = part
    else:
        @pl.when(j % r == 0)
        def _():
            sh_ref[0, 0] = part

        @pl.when(j % r != 0)
        def _():
            sh_ref[0, 0] += part

    cos_t = cos_ref[...]
    sin_t = sin_ref[...]
    hl = lax.broadcasted_iota(jnp.int32, sin_t.shape, 1) % DIFF_HEAD_DIM
    sa = jnp.where(hl < ROT_DIM // 2, -sin_t, 0.0)
    sb = jnp.where(jnp.logical_and(hl >= ROT_DIM // 2, hl < ROT_DIM), sin_t, 0.0)

    def rope(xg):
        return xg * cos_t + pltpu.roll(xg, LANES - ROT_DIM // 2, axis=1) * sa + pltpu.roll(xg, ROT_DIM // 2, axis=1) * sb

    pq = jnp.dot(hb, w_ref[:, hw3:hw3 + ATTN_WIDTH], preferred_element_type=F32)
    pk = jnp.dot(hb, w_ref[:, hw3 + ATTN_WIDTH:hw3 + 2 * ATTN_WIDTH], preferred_element_type=F32)
    for g in range(ATTN_WIDTH // LANES):
        sl = slice(g * LANES, (g + 1) * LANES)
        q_ref[0, :, sl] = (rope(pq[:, sl]) * (math.log2(math.e) * DIFF_HEAD_DIM ** -0.5)).astype(BF16)
        k_ref[0, :, sl] = rope(pk[:, sl]).astype(BF16)
    pv = jnp.dot(hb, w_ref[:, hw3 + 2 * ATTN_WIDTH:], preferred_element_type=F32).astype(BF16)
    ones = jnp.ones((pv.shape[0], DIFF_V_DIM), BF16)
    for g in range(N_DIFF_HEADS):
        v_ref[0, :, 2 * g * DIFF_V_DIM:(2 * g + 1) * DIFF_V_DIM] = pv[:, g * DIFF_V_DIM:(g + 1) * DIFF_V_DIM]
        v_ref[0, :, (2 * g + 1) * DIFF_V_DIM:(2 * g + 2) * DIFF_V_DIM] = ones


def _inproj(x, mod, norm_w, w_in_bf, cos_t, sin_t, conv_w, conv_b, fwd_bf, tm, p):
    b, l, d = x.shape
    c = HYENA_WIDTH
    hw3 = 3 * c
    nt = l // tm
    r = p // tm
    rh = tm // HALO
    nrowh = l // HALO
    tok = lambda w: pl.BlockSpec((1, tm, w), lambda i, j: (i, j, 0))
    tab = pl.BlockSpec((tm, LANES), lambda i, j: (j, 0))
    return pl.pallas_call(
        functools.partial(_inproj_kernel, nt=nt, r=r),
        grid=(b, nt),
        in_specs=[tok(d),
                  pl.BlockSpec((1, HALO, d), lambda i, j: (i, jnp.maximum(j * rh - 1, 0), 0)),
                  pl.BlockSpec((1, HALO, d), lambda i, j: (i, jnp.minimum((j + 1) * rh, nrowh - 1), 0)),
                  pl.BlockSpec((1, 6, d), lambda i, j: (i, 0, 0)),
                  _const_spec((1, d)),
                  _const_spec((d, D_IN)),
                  tab, tab,
                  _const_spec((3, hw3)), _const_spec((1, hw3)),
                  pl.BlockSpec((2 * p, tm), lambda i, j: (0, j % r))],
        out_specs=[pl.BlockSpec((1, 1, 2 * p, c), lambda i, j: (i, j // r, 0, 0)),
                   tok(c), tok(c), tok(ATTN_WIDTH), tok(ATTN_WIDTH), tok(2 * ATTN_WIDTH)],
        out_shape=[jax.ShapeDtypeStruct((b, l // p, 2 * p, c), F32),
                   jax.ShapeDtypeStruct((b, l, c), BF16),
                   jax.ShapeDtypeStruct((b, l, c), BF16),
                   jax.ShapeDtypeStruct((b, l, ATTN_WIDTH), BF16),
                   jax.ShapeDtypeStruct((b, l, ATTN_WIDTH), BF16),
                   jax.ShapeDtypeStruct((b, l, 2 * ATTN_WIDTH), BF16)],
        compiler_params=_cparams(("parallel", "arbitrary")),
        name="inproj",
    )(x, x, x, mod, norm_w.reshape(1, d), w_in_bf, cos_t, sin_t, conv_w, conv_b.reshape(1, hw3), fwd_bf)


def _rope_tables(l):
    inv_freq = ROPE_THETA ** (-jnp.arange(0, ROT_DIM, 2, dtype=F32) / ROT_DIM)
    hl = np.arange(LANES) % DIFF_HEAD_DIM
    lane_freq = jnp.where(hl < ROT_DIM, inv_freq[hl % (ROT_DIM // 2)], 0.0)
    ang = jnp.arange(l, dtype=F32)[:, None] * lane_freq[None, :]
    return jnp.cos(ang), jnp.sin(ang)


def _dft_mats(p):
    f = np.arange(p, dtype=np.float64)
    n = np.arange(p, dtype=np.float64)
    th = np.pi * (2.0 * f[:, None] + 1.0) * n[None, :] / (2.0 * p)
    fwd = np.concatenate([np.cos(th), -np.sin(th)], axis=0)
    n2 = np.arange(2 * p, dtype=np.float64)
    th2 = np.pi * (2.0 * f[None, :] + 1.0) * n2[:, None] / (2.0 * p)
    inv = np.concatenate([np.cos(th2), -np.sin(th2)], axis=1) / p
    fwd2 = np.stack([fwd, fwd[:, ::-1]], axis=0).astype(np.float32)
    inv2 = np.stack([inv[:p], inv[p:]], axis=0).astype(np.float32)
    return jnp.asarray(fwd2), jnp.asarray(inv2)


def _filt_kernel(fb_ref, w1t_ref, w1c_ref, w1s_ref, b1_ref, w2_ref, b2_ref, w3_ref, b3_ref, w4h_ref, w4b_ref,
                 fr_ref, ad_ref, mat_ref, o_ref, *, l, p, nb):
    j = pl.program_id(0)
    neg = j < nb
    e = jnp.where(neg, nb - 1 - j, j - nb)
    base = e * p + jnp.where(neg, 1, 0)
    posl = (base + lax.broadcasted_iota(jnp.int32, (1, p), 1)).astype(F32)
    tl = posl / float(l - 1)
    wl = (2.0 * math.pi) * posl / float(l)
    ang = fb_ref[...] * wl
    fr = fr_ref[...]
    dot = functools.partial(jnp.dot, precision=HIGHEST, preferred_element_type=F32)
    z1 = w1t_ref[...] * tl + dot(w1c_ref[...], jnp.cos(ang)) - dot(w1s_ref[...], jnp.sin(ang))
    h = jnp.sin(fr * (z1 + b1_ref[...]))
    h = jnp.sin(fr * (dot(w2_ref[...], h) + b2_ref[...]))
    h = jnp.sin(fr * (dot(w3_ref[...], h) + b3_ref[...]))
    row = lax.broadcasted_iota(jnp.int32, (p, 1), 0)
    pos = base + row
    t = pos.astype(F32) / float(l - 1)
    kap = lax.dot_general(h, w4h_ref[...], (((0,), (0,)), ((), ())), precision=HIGHEST,
                          preferred_element_type=F32) * jnp.exp(-t * ad_ref[...])
    kap = jnp.where(pos < l, kap, 0.0)
    hb0 = jnp.sum(h[:, 0:1] * w4b_ref[...], axis=0, keepdims=True)
    kap = kap + jnp.where(jnp.logical_and(row == 0, j == nb), hb0, 0.0)
    o_ref[0] = jnp.dot(mat_ref[0], kap.astype(BF16), preferred_element_type=F32)


def _filter_blocks(l, p, fw1, fb1, fw2, fb2, fw3, fb3, fw4, ffreq, fwd2_bf):
    nb = l // p
    c = HYENA_WIDTH
    bands = (FILTER_EMB - 1) // 2
    fbands = jnp.linspace(1e-4, bands - 1, bands, dtype=F32)[:, None]
    col = lambda a: a.reshape(-1, 1)
    min_decay = math.log(DECAY_TARGET) / SLOW_DECAY_PCT
    max_decay = math.log(DECAY_TARGET) / FAST_DECAY_PCT
    absd = jnp.abs(jnp.linspace(min_decay, max_decay, c, dtype=F32))[None, :]
    half = lambda j: jnp.where(j < nb, 1, 0)
    args = (fbands, col(fw1[0]), fw1[1:1 + bands].T, fw1[1 + bands:].T, col(fb1), fw2.T, col(fb2),
            fw3.T, col(fb3), fw4, fw4[:, c:], col(ffreq), absd, fwd2_bf)
    specs = [_const_spec(a.shape) for a in args]
    specs[9] = pl.BlockSpec((FILTER_ORDER, c), lambda j: (0, half(j)))
    specs[13] = pl.BlockSpec((1, 2 * p, p), lambda j: (half(j), 0, 0))
    return pl.pallas_call(
        functools.partial(_filt_kernel, l=l, p=p, nb=nb),
        grid=(2 * nb,),
        in_specs=specs,
        out_specs=pl.BlockSpec((1, 2 * p, c), lambda j: (j, 0, 0)),
        out_shape=jax.ShapeDtypeStruct((2 * nb, 2 * p, c), F32),
        compiler_params=_cparams(("parallel",)),
        name="filt",
    )(*args)


def _facc_kernel(s_ref, k_ref, z_ref, ss_ref, ks_ref, *, nb, rt):
    bt, _, _, ft, ct = s_ref.shape
    ss_ref[...] = s_ref[:, :, 0] + s_ref[:, :, 1]
    ks_ref[...] = k_ref[:, 0] + k_ref[:, 1]

    def per_tile(ti, carry0):
        rows = pl.ds(pl.multiple_of(ti * rt, rt), rt)

        def per_e(e, carry):
            def per_a(a, acc):
                t1, t2, t3 = acc
                j = e - 1 - a + nb
                t1 = t1 + k_ref[j, 0, rows, :][None] * s_ref[:, a, 0, rows, :]
                t2 = t2 + k_ref[j, 1, rows, :][None] * s_ref[:, a, 1, rows, :]
                t3 = t3 + ks_ref[j, rows, :][None] * ss_ref[:, a, rows, :]
                return (t1, t2, t3)

            zero = jnp.zeros((bt, rt, ct), F32)
            t1, t2, t3 = lax.fori_loop(0, nb, per_a, (zero, zero, zero), unroll=min(nb, 4))
            z_ref[:, e, 0, rows, :] = t1 - t2
            z_ref[:, e, 1, rows, :] = t3 - t1 - t2
            return carry

        lax.fori_loop(0, nb + 1, per_e, 0)
        return carry0

    lax.fori_loop(0, ft // rt, per_tile, 0)


def _facc(sh, kh, p, bt, ft):
    b, nb = sh.shape[0], sh.shape[1]
    c = HYENA_WIDTH
    ct = LANES
    s5 = sh.reshape(b, nb, 2, p, c)
    k4 = kh.reshape(2 * nb, 2, p, c)
    rt = min(ft, (8 * SUBLANES) // bt)
    z = pl.pallas_call(
        functools.partial(_facc_kernel, nb=nb, rt=rt),
        grid=(c // ct, p // ft, b // bt),
        in_specs=[pl.BlockSpec((bt, nb, 2, ft, ct), lambda ci, fi, bi: (bi, 0, 0, fi, ci)),
                  pl.BlockSpec((2 * nb, 2, ft, ct), lambda ci, fi, bi: (0, 0, fi, ci))],
        out_specs=pl.BlockSpec((bt, nb + 1, 2, ft, ct), lambda ci, fi, bi: (bi, 0, 0, fi, ci)),
        out_shape=jax.ShapeDtypeStruct((b, nb + 1, 2, p, c), F32),
        scratch_shapes=[pltpu.VMEM((bt, nb, ft, ct), F32), pltpu.VMEM((2 * nb, ft, ct), F32)],
        compiler_params=_cparams(("parallel", "parallel", "parallel")),
        name="facc",
    )(s5, k4)
    return z.reshape(b, nb + 1, 2 * p, c)


def _hyout_kernel(z_ref, inv_ref, x0_ref, s_ref, bias_ref, nw_ref, o_ref, tail_ref):
    g = pl.program_id(1)
    zb = z_ref[0, 0].astype(BF16)

    @pl.when(g > 0)
    def _():
        y = tail_ref[...] + jnp.dot(inv_ref[0], zb, preferred_element_type=F32)
        y = x0_ref[0].astype(F32) * (y + s_ref[0].astype(F32) * bias_ref[...])
        ms = jnp.mean(y * y, axis=-1, keepdims=True)
        o_ref[0] = (y * lax.rsqrt(ms + NORM_EPS) * nw_ref[...]).astype(o_ref.dtype)

    @pl.when(g < pl.num_programs(1) - 1)
    def _():
        tail_ref[...] = jnp.dot(inv_ref[1], zb, preferred_element_type=F32)


def _hyout(zh, inv2_bf, x0, s, hbias, hnorm_w, p):
    b, l, c = x0.shape
    nb = l // p
    tok = pl.BlockSpec((1, p, c), lambda i, g: (i, jnp.maximum(g - 1, 0), 0))
    return pl.pallas_call(
        _hyout_kernel,
        grid=(b, nb + 1),
        in_specs=[pl.BlockSpec((1, 1, 2 * p, c), lambda i, g: (i, g, 0, 0)),
                  _const_spec((2, p, 2 * p)),
                  tok, tok, _const_spec((1, c)), _const_spec((1, c))],
        out_specs=tok,
        out_shape=jax.ShapeDtypeStruct((b, l, c), BF16),
        scratch_shapes=[pltpu.VMEM((p, c), F32)],
        compiler_params=_cparams(("parallel", "arbitrary")),
        name="hyout",
    )(zh, inv2_bf, x0, s, hbias.reshape(1, c), hnorm_w.reshape(1, c))


def _attn_kernel(q_ref, k_ref, v_ref, lq1_ref, lk1_ref, lq2_ref, lk2_ref, sw_ref, o_ref,
                 qs_ref, m_ref, acc_ref, *, tq, tk, nk, nch):
    lane = lax.broadcasted_iota(jnp.int32, (tq, LANES), 1)
    for c in range(nch):
        q = q_ref[0, c * tq:(c + 1) * tq, :]
        zero = jnp.zeros_like(q)
        qs_ref[c, 0:tq, :] = jnp.where(lane < DIFF_HEAD_DIM, q, zero)
        qs_ref[c, tq:2 * tq, :] = jnp.where(lane >= DIFF_HEAD_DIM, q, zero)
    m_ref[...] = jnp.full(m_ref.shape, -jnp.inf, F32)
    acc_ref[...] = jnp.zeros(acc_ref.shape, F32)
    nt = tk // LANES

    def step(i, carry):
        off = pl.multiple_of(i * tk, tk)
        kb = k_ref[0, pl.ds(off, tk), :]
        vb = v_ref[0, pl.ds(off, tk), :]
        for c in range(nch):
            s = lax.dot_general(qs_ref[c], kb, (((1,), (1,)), ((), ())), preferred_element_type=F32)
            st = [s[:, t * LANES:(t + 1) * LANES] for t in range(nt)]
            mt = st[0]
            for t in range(1, nt):
                mt = jnp.maximum(mt, st[t])
            m_prev = m_ref[c]
            m_new = jnp.maximum(m_prev, jnp.max(mt, axis=-1, keepdims=True))
            alpha = jnp.exp2(m_prev - m_new)
            pexp = jnp.concatenate([jnp.exp2(st[t] - m_new).astype(BF16) for t in range(nt)], axis=1)
            pv = jnp.dot(pexp, vb, preferred_element_type=F32)
            acc_ref[c, :, 0:LANES] = alpha * acc_ref[c, :, 0:LANES] + pv[:, 0:LANES]
            acc_ref[c, :, LANES:2 * LANES] = alpha * acc_ref[c, :, LANES:2 * LANES] + pv[:, LANES:2 * LANES]
            m_ref[c] = m_new
        return carry

    lax.fori_loop(0, nk, step, 0, unroll=8)

    lam = (jnp.exp(jnp.sum(lq1_ref[...] * lk1_ref[...], axis=-1, keepdims=True))
           - jnp.exp(jnp.sum(lq2_ref[...] * lk2_ref[...], axis=-1, keepdims=True)) + LAMBDA_INIT)
    for c in range(nch):
        o = acc_ref[c, :, 0:LANES] / acc_ref[c, :, LANES:2 * LANES]
        o = o[0:tq, :] - lam * o[tq:2 * tq, :]
        ms = jnp.mean(o * o, axis=-1, keepdims=True)
        o = (o * lax.rsqrt(ms + SUBLN_EPS) * sw_ref[...]) * (1.0 - LAMBDA_INIT)
        o_ref[0, c * tq:(c + 1) * tq, :] = o.astype(o_ref.dtype)


def _attn(q, k, v1, lq1, lk1, lq2, lk2, subln_w, tq, tk, nch):
    b, l, _ = q.shape
    nk = l // tk
    vec = lambda a: a.reshape(1, -1)
    k_spec = pl.BlockSpec((1, l, LANES), lambda i, h, j: (i, 0, h))
    v_spec = pl.BlockSpec((1, l, 2 * LANES), lambda i, h, j: (i, 0, h))
    q_spec = pl.BlockSpec((1, nch * tq, LANES), lambda i, h, j: (i, j, h))
    return pl.pallas_call(
        functools.partial(_attn_kernel, tq=tq, tk=tk, nk=nk, nch=nch),
        grid=(b, N_DIFF_HEADS, l // (nch * tq)),
        in_specs=[q_spec, k_spec, v_spec,
                  _const_spec((1, DIFF_HEAD_DIM)), _const_spec((1, DIFF_HEAD_DIM)),
                  _const_spec((1, DIFF_HEAD_DIM)), _const_spec((1, DIFF_HEAD_DIM)),
                  _const_spec((1, DIFF_V_DIM))],
        out_specs=q_spec,
        out_shape=jax.ShapeDtypeStruct((b, l, ATTN_WIDTH), BF16),
        scratch_shapes=[pltpu.VMEM((nch, 2 * tq, LANES), BF16),
                        pltpu.VMEM((nch, 2 * tq, LANES), F32),
                        pltpu.VMEM((nch, 2 * tq, 2 * LANES), F32)],
        compiler_params=_cparams(("parallel", "parallel", "parallel")),
        name="attn",
    )(q, k, v1, vec(lq1), vec(lk1), vec(lq2), vec(lk2), vec(subln_w))


def _post_kernel(x_ref, hy_ref, at_ref, mod_ref, woh_ref, woa_ref, n2_ref, w1_ref, w2_ref, fw_ref, o_ref, *, ffc):
    x = x_ref[0]
    g1 = mod_ref[0, 2:3, :]
    sh2 = mod_ref[0, 3:4, :]
    sc2 = mod_ref[0, 4:5, :]
    g2 = mod_ref[0, 5:6, :]
    mix = (jnp.dot(hy_ref[0], woh_ref[...], preferred_element_type=F32)
           + jnp.dot(at_ref[0], woa_ref[...], preferred_element_type=F32))
    x1 = x + g1 * mix
    ms = jnp.mean(x1 * x1, axis=-1, keepdims=True)
    hb = ((x1 * lax.rsqrt(ms + NORM_EPS) * n2_ref[...]) * (1.0 + sc2) + sh2).astype(BF16)
    acc = jnp.zeros_like(x1)
    for c0 in range(0, D_FF, ffc):
        hc = jnp.dot(hb, w1_ref[:, c0:c0 + ffc], preferred_element_type=F32)
        hc = jnp.square(jnp.maximum(hc, 0.0))
        acc = acc + jnp.dot(hc.astype(BF16), w2_ref[c0:c0 + ffc, :], preferred_element_type=F32)
    x2 = x1 + g2 * acc
    ms2 = jnp.mean(x2 * x2, axis=-1, keepdims=True)
    o_ref[0] = x2 * lax.rsqrt(ms2 + NORM_EPS) * fw_ref[...]


def _post(x, hy, at, mod, wo_bf, norm2_w, w1_bf, w2_bf, final_w, tm):
    b, l, d = x.shape
    tok = lambda w: pl.BlockSpec((1, tm, w), lambda i, j: (i, j, 0))
    return pl.pallas_call(
        functools.partial(_post_kernel, ffc=1024),
        grid=(b, l // tm),
        in_specs=[tok(d), tok(HYENA_WIDTH), tok(ATTN_WIDTH),
                  pl.BlockSpec((1, 6, d), lambda i, j: (i, 0, 0)),
                  _const_spec((HYENA_WIDTH, d)), _const_spec((ATTN_WIDTH, d)),
                  _const_spec((1, d)), _const_spec((d, D_FF)), _const_spec((D_FF, d)), _const_spec((1, d))],
        out_specs=tok(d),
        out_shape=jax.ShapeDtypeStruct((b, l, d), F32),
        compiler_params=_cparams(("parallel", "parallel")),
        name="post",
    )(x, hy, at, mod, wo_bf[:HYENA_WIDTH], wo_bf[HYENA_WIDTH:], norm2_w.reshape(1, d), w1_bf, w2_bf,
      final_w.reshape(1, d))


def _tiles(b, l):
    tm = min(512, l)
    p = 1024 if l >= 8192 else min(512, l // 2)
    nch = 4
    tq = min(256, l // nch)
    tk = min(512, l)
    nb = l // p
    bt = math.gcd(b, 2)
    per_ft = LANES * 4 * 2 * (bt * nb + 2 * nb + bt * (nb + 1))
    ft = SUBLANES
    while ft * 2 <= min(p, 128) and per_ft * ft * 2 <= 8 * 1024 * 1024:
        ft *= 2
    return dict(tm=tm, p=p, tq=tq, tk=tk, nch=nch, bt=bt, ft=ft)


def _encoder(x, mod, w, tables):
    b, l, d = x.shape
    t = _tiles(b, l)
    p = t["p"]
    fwd2_bf, inv2_bf = tables["fwd2"], tables["inv2"]
    cos_t, sin_t = _rope_tables(l)
    sh, x0, s, q, k, v = _inproj(x, mod, w["norm1_w"], w["w_in"], cos_t, sin_t, w["conv_w"], w["conv_b"],
                                 fwd2_bf[0], t["tm"], p)
    kh = _filter_blocks(l, p, w["filt_w1"], w["filt_b1"], w["filt_w2"], w["filt_b2"], w["filt_w3"], w["filt_b3"],
                        w["filt_w4"], w["filt_freq"], fwd2_bf)
    zh = _facc(sh, kh, p, t["bt"], t["ft"])
    hy = _hyout(zh, inv2_bf, x0, s, w["hyena_bias"], w["hyena_norm_w"], p)
    at = _attn(q, k, v, w["lambda_q1"], w["lambda_k1"], w["lambda_q2"], w["lambda_k2"], w["subln_w"],
               t["tq"], t["tk"], t["nch"])
    return _post(x, hy, at, mod, w["w_out"], w["norm2_w"], w["w_mlp1"], w["w_mlp2"], w["final_w"], t["tm"])


def kernel(x_prompt, x_sample, c_prompt, c_sample, w_ada, b_ada, norm1_w, w_in, conv_w, conv_b, filt_w1, filt_b1, filt_w2, filt_b2, filt_w3, filt_b3, filt_w4, filt_freq, hyena_bias, hyena_norm_w, lambda_q1, lambda_k1, lambda_q2, lambda_k2, subln_w, w_out, norm2_w, w_mlp1, w_mlp2, final_w):
    assert w_ada.shape[0] == 1, "single-layer encoder"
    w = dict(norm1_w=norm1_w[0], w_in=w_in[0].astype(BF16), conv_w=conv_w[0], conv_b=conv_b[0],
             filt_w1=filt_w1[0], filt_b1=filt_b1[0], filt_w2=filt_w2[0], filt_b2=filt_b2[0],
             filt_w3=filt_w3[0], filt_b3=filt_b3[0], filt_w4=filt_w4[0], filt_freq=filt_freq[0],
             hyena_bias=hyena_bias[0], hyena_norm_w=hyena_norm_w[0], lambda_q1=lambda_q1[0],
             lambda_k1=lambda_k1[0], lambda_q2=lambda_q2[0], lambda_k2=lambda_k2[0], subln_w=subln_w[0],
             w_out=w_out[0].astype(BF16), norm2_w=norm2_w[0], w_mlp1=w_mlp1[0].astype(BF16),
             w_mlp2=w_mlp2[0].astype(BF16), final_w=final_w)
    nbp = c_prompt.shape[0]
    mod = _ada(jnp.concatenate([c_prompt, c_sample], axis=0), w_ada[0], b_ada[0])
    mod = mod.reshape(mod.shape[0], 6, D_MODEL)
    outs = []
    for x, m in ((x_prompt, mod[:nbp]), (x_sample, mod[nbp:])):
        p = _tiles(*x.shape[:2])["p"]
        fwd2, inv2 = _dft_mats(p)
        tables = dict(fwd2=fwd2.astype(BF16), inv2=inv2.astype(BF16))
        outs.append(_encoder(x, m, w, tables))
    return tuple(outs)
```

```python
import functools
import math

import numpy as np
import jax
import jax.numpy as jnp
from jax import lax
from jax.experimental import pallas as pl
from jax.experimental.pallas import tpu as pltpu

F32 = jnp.float32
BF16 = jnp.bfloat16

D_MODEL = 1024
HYENA_WIDTH = 512
ATTN_WIDTH = 512
N_DIFF_HEADS = 4
DIFF_HEAD_DIM = 64
DIFF_V_DIM = 2 * DIFF_HEAD_DIM
ROT_DIM = DIFF_HEAD_DIM // 4
ROPE_THETA = 500000.0
D_IN = 3 * HYENA_WIDTH + 3 * ATTN_WIDTH
D_FF = 4 * D_MODEL
FILTER_EMB = 33
FILTER_ORDER = 64
FAST_DECAY_PCT = 0.3
SLOW_DECAY_PCT = 1.5
DECAY_TARGET = 1e-2
NORM_EPS = 1e-6
SUBLN_EPS = 1e-5
LAMBDA_INIT = 0.8 - 0.6 * math.exp(-0.3 * 0)

LANES = 128
SUBLANES = 8
HALO = 2 * SUBLANES
VMEM_LIMIT = 56 * 1024 * 1024

HIGHEST = lax.Precision.HIGHEST


def _cparams(sem):
    return pltpu.CompilerParams(dimension_semantics=sem, vmem_limit_bytes=VMEM_LIMIT)


def _const_spec(shape):
    nd = len(shape)
    return pl.BlockSpec(shape, lambda *_: (0,) * nd, pipeline_mode=pl.Buffered(1))


def _ada_kernel(c_ref, w_ref, b_ref, o_ref):
    c = c_ref[...]
    s = c * jax.nn.sigmoid(c)
    o_ref[...] = jnp.dot(s.astype(BF16), w_ref[...].astype(BF16), preferred_element_type=F32) + b_ref[...]


def _ada(c, w_ada, b_ada):
    nb_, d = c.shape
    n = w_ada.shape[1]
    tn = 1024
    return pl.pallas_call(
        _ada_kernel,
        grid=(n // tn,),
        in_specs=[pl.BlockSpec((nb_, d), lambda j: (0, 0)),
                  pl.BlockSpec((d, tn), lambda j: (0, j)),
                  pl.BlockSpec((1, tn), lambda j: (0, j))],
        out_specs=pl.BlockSpec((nb_, tn), lambda j: (0, j)),
        out_shape=jax.ShapeDtypeStruct((nb_, n), F32),
        compiler_params=_cparams(("parallel",)),
        name="ada",
    )(c, w_ada, b_ada.reshape(1, n))


def _inproj_kernel(x_ref, xp_ref, xn_ref, mod_ref, nw_ref, w_ref, cos_ref, sin_ref, cw_ref, cb_ref, fwd_ref,
                   sh_ref, x0_ref, s_ref, q_ref, k_ref, v_ref, *, nt, r):
    j = pl.program_id(1)
    tm = x_ref.shape[1]
    c = HYENA_WIDTH
    hw3 = 3 * c
    shift = mod_ref[0, 0:1, :]
    scale = mod_ref[0, 1:2, :]

    def normmod(x):
        ms = jnp.mean(x * x, axis=-1, keepdims=True)
        return ((x * lax.rsqrt(ms + NORM_EPS) * nw_ref[...]) * (1.0 + scale) + shift).astype(BF16)

    hb = normmod(x_ref[0])
    hp = jnp.where(j > 0, normmod(xp_ref[0]), jnp.zeros((HALO, hb.shape[1]), BF16))
    hn = jnp.where(j < nt - 1, normmod(xn_ref[0]), jnp.zeros((HALO, hb.shape[1]), BF16))
    he = jnp.concatenate([hp, hb, hn], axis=0)
    cvs = []
    for c0 in range(0, hw3, c):
        ue = jnp.dot(he, w_ref[:, c0:c0 + c], preferred_element_type=F32)
        cvs.append(ue[HALO - 1:HALO - 1 + tm] * cw_ref[0:1, c0:c0 + c] + ue[HALO:HALO + tm] * cw_ref[1:2, c0:c0 + c]
                   + ue[HALO + 1:HALO + 1 + tm] * cw_ref[2:3, c0:c0 + c] + cb_ref[:, c0:c0 + c])
    s = (cvs[1] * cvs[2]).astype(BF16)
    x0_ref[0] = cvs[0].astype(BF16)
    s_ref[0] = s
    part = jnp.dot(fwd_ref[...], s, preferred_element_type=F32)
    if r == 1:
        sh_ref[0, 0] = part
    else:
        @pl.when(j % r == 0)
        def _():
            sh_ref[0, 0] = part

        @pl.when(j % r != 0)
        def _():
            sh_ref[0, 0] += part

    cos_t = cos_ref[...]
    sin_t = sin_ref[...]
    hl = lax.broadcasted_iota(jnp.int32, sin_t.shape, 1) % DIFF_HEAD_DIM
    sa = jnp.where(hl < ROT_DIM // 2, -sin_t, 0.0)
    sb = jnp.where(jnp.logical_and(hl >= ROT_DIM // 2, hl < ROT_DIM), sin_t, 0.0)

    def rope(xg):
        return xg * cos_t + pltpu.roll(xg, LANES - ROT_DIM // 2, axis=1) * sa + pltpu.roll(xg, ROT_DIM // 2, axis=1) * sb

    pq = jnp.dot(hb, w_ref[:, hw3:hw3 + ATTN_WIDTH], preferred_element_type=F32)
    pk = jnp.dot(hb, w_ref[:, hw3 + ATTN_WIDTH:hw3 + 2 * ATTN_WIDTH], preferred_element_type=F32)
    for g in range(ATTN_WIDTH // LANES):
        sl = slice(g * LANES, (g + 1) * LANES)
        q_ref[0, :, sl] = (rope(pq[:, sl]) * (math.log2(math.e) * DIFF_HEAD_DIM ** -0.5)).astype(BF16)
        k_ref[0, :, sl] = rope(pk[:, sl]).astype(BF16)
    pv = jnp.dot(hb, w_ref[:, hw3 + 2 * ATTN_WIDTH:], preferred_element_type=F32).astype(BF16)
    ones = jnp.ones((pv.shape[0], DIFF_V_DIM), BF16)
    for g in range(N_DIFF_HEADS):
        v_ref[0, :, 2 * g * DIFF_V_DIM:(2 * g + 1) * DIFF_V_DIM] = pv[:, g * DIFF_V_DIM:(g + 1) * DIFF_V_DIM]
        v_ref[0, :, (2 * g + 1) * DIFF_V_DIM:(2 * g + 2) * DIFF_V_DIM] = ones


def _inproj(x, mod, norm_w, w_in_bf, cos_t, sin_t, conv_w, conv_b, fwd_bf, tm, p):
    b, l, d = x.shape
    c = HYENA_WIDTH
    hw3 = 3 * c
    nt = l // tm
    r = p // tm
    rh = tm // HALO
    nrowh = l // HALO
    tok = lambda w: pl.BlockSpec((1, tm, w), lambda i, j: (i, j, 0))
    tab = pl.BlockSpec((tm, LANES), lambda i, j: (j, 0))
    return pl.pallas_call(
        functools.partial(_inproj_kernel, nt=nt, r=r),
        grid=(b, nt),
        in_specs=[tok(d),
                  pl.BlockSpec((1, HALO, d), lambda i, j: (i, jnp.maximum(j * rh - 1, 0), 0)),
                  pl.BlockSpec((1, HALO, d), lambda i, j: (i, jnp.minimum((j + 1) * rh, nrowh - 1), 0)),
                  pl.BlockSpec((1, 6, d), lambda i, j: (i, 0, 0)),
                  _const_spec((1, d)),
                  _const_spec((d, D_IN)),
                  tab, tab,
                  _const_spec((3, hw3)), _const_spec((1, hw3)),
                  pl.BlockSpec((2 * p, tm), lambda i, j: (0, j % r))],
        out_specs=[pl.BlockSpec((1, 1, 2 * p, c), lambda i, j: (i, j // r, 0, 0)),
                   tok(c), tok(c), tok(ATTN_WIDTH), tok(ATTN_WIDTH), tok(2 * ATTN_WIDTH)],
        out_shape=[jax.ShapeDtypeStruct((b, l // p, 2 * p, c), F32),
                   jax.ShapeDtypeStruct((b, l, c), BF16),
                   jax.ShapeDtypeStruct((b, l, c), BF16),
                   jax.ShapeDtypeStruct((b, l, ATTN_WIDTH), BF16),
                   jax.ShapeDtypeStruct((b, l, ATTN_WIDTH), BF16),
                   jax.ShapeDtypeStruct((b, l, 2 * ATTN_WIDTH), BF16)],
        compiler_params=_cparams(("parallel", "arbitrary")),
        name="inproj",
    )(x, x, x, mod, norm_w.reshape(1, d), w_in_bf, cos_t, sin_t, conv_w, conv_b.reshape(1, hw3), fwd_bf)


def _rope_tables(l):
    inv_freq = ROPE_THETA ** (-jnp.arange(0, ROT_DIM, 2, dtype=F32) / ROT_DIM)
    hl = np.arange(LANES) % DIFF_HEAD_DIM
    lane_freq = jnp.where(hl < ROT_DIM, inv_freq[hl % (ROT_DIM // 2)], 0.0)
    ang = jnp.arange(l, dtype=F32)[:, None] * lane_freq[None, :]
    return jnp.cos(ang), jnp.sin(ang)


def _dft_mats(p):
    f = np.arange(p, dtype=np.float64)
    n = np.arange(p, dtype=np.float64)
    th = np.pi * (2.0 * f[:, None] + 1.0) * n[None, :] / (2.0 * p)
    fwd = np.concatenate([np.cos(th), -np.sin(th)], axis=0)
    n2 = np.arange(2 * p, dtype=np.float64)
    th2 = np.pi * (2.0 * f[None, :] + 1.0) * n2[:, None] / (2.0 * p)
    inv = np.concatenate([np.cos(th2), -np.sin(th2)], axis=1) / p
    fwd2 = np.stack([fwd, fwd[:, ::-1]], axis=0).astype(np.float32)
    inv2 = np.stack([inv[:p], inv[p:]], axis=0).astype(np.float32)
    return jnp.asarray(fwd2), jnp.asarray(inv2)


def _filt_kernel(fb_ref, w1t_ref, w1c_ref, w1s_ref, b1_ref, w2_ref, b2_ref, w3_ref, b3_ref, w4h_ref, w4b_ref,
                 fr_ref, ad_ref, mat_ref, o_ref, *, l, p, nb):
    j = pl.program_id(0)
    neg = j < nb
    e = jnp.where(neg, nb - 1 - j, j - nb)
    base = e * p + jnp.where(neg, 1, 0)
    posl = (base + lax.broadcasted_iota(jnp.int32, (1, p), 1)).astype(F32)
    tl = posl / float(l - 1)
    wl = (2.0 * math.pi) * posl / float(l)
    ang = fb_ref[...] * wl
    fr = fr_ref[...]
    dot = functools.partial(jnp.dot, precision=HIGHEST, preferred_element_type=F32)
    z1 = w1t_ref[...] * tl + dot(w1c_ref[...], jnp.cos(ang)) - dot(w1s_ref[...], jnp.sin(ang))
    h = jnp.sin(fr * (z1 + b1_ref[...]))
    h = jnp.sin(fr * (dot(w2_ref[...], h) + b2_ref[...]))
    h = jnp.sin(fr * (dot(w3_ref[...], h) + b3_ref[...]))
    row = lax.broadcasted_iota(jnp.int32, (p, 1), 0)
    pos = base + row
    t = pos.astype(F32) / float(l - 1)
    kap = lax.dot_general(h, w4h_ref[...], (((0,), (0,)), ((), ())), precision=HIGHEST,
                          preferred_element_type=F32) * jnp.exp(-t * ad_ref[...])
    kap = jnp.where(pos < l, kap, 0.0)
    hb0 = jnp.sum(h[:, 0:1] * w4b_ref[...], axis=0, keepdims=True)
    kap = kap + jnp.where(jnp.logical_and(row == 0, j == nb), hb0, 0.0)
    o_ref[0] = jnp.dot(mat_ref[0], kap.astype(BF16), preferred_element_type=F32)


def _filter_blocks(l, p, fw1, fb1, fw2, fb2, fw3, fb3, fw4, ffreq, fwd2_bf):
    nb = l // p
    c = HYENA_WIDTH
    bands = (FILTER_EMB - 1) // 2
    fbands = jnp.linspace(1e-4, bands - 1, bands, dtype=F32)[:, None]
    col = lambda a: a.reshape(-1, 1)
    min_decay = math.log(DECAY_TARGET) / SLOW_DECAY_PCT
    max_decay = math.log(DECAY_TARGET) / FAST_DECAY_PCT
    absd = jnp.abs(jnp.linspace(min_decay, max_decay, c, dtype=F32))[None, :]
    half = lambda j: jnp.where(j < nb, 1, 0)
    args = (fbands, col(fw1[0]), fw1[1:1 + bands].T, fw1[1 + bands:].T, col(fb1), fw2.T, col(fb2),
            fw3.T, col(fb3), fw4, fw4[:, c:], col(ffreq), absd, fwd2_bf)
    specs = [_const_spec(a.shape) for a in args]
    specs[9] = pl.BlockSpec((FILTER_ORDER, c), lambda j: (0, half(j)))
    specs[13] = pl.BlockSpec((1, 2 * p, p), lambda j: (half(j), 0, 0))
    return pl.pallas_call(
        functools.partial(_filt_kernel, l=l, p=p, nb=nb),
        grid=(2 * nb,),
        in_specs=specs,
        out_specs=pl.BlockSpec((1, 2 * p, c), lambda j: (j, 0, 0)),
        out_shape=jax.ShapeDtypeStruct((2 * nb, 2 * p, c), F32),
        compiler_params=_cparams(("parallel",)),
        name="filt",
    )(*args)


def _hyout_kernel(z_ref, inv_ref, x0_ref, s_ref, bias_ref, nw_ref, o_ref, tail_ref):
    g = pl.program_id(1)
    zb = z_ref[0, 0].astype(BF16)

    @pl.when(g > 0)
    def _():
        y = tail_ref[...] + jnp.dot(inv_ref[0], zb, preferred_element_type=F32)
        y = x0_ref[0].astype(F32) * (y + s_ref[0].astype(F32) * bias_ref[...])
        ms = jnp.mean(y * y, axis=-1, keepdims=True)
        o_ref[0] = (y * lax.rsqrt(ms + NORM_EPS) * nw_ref[...]).astype(o_ref.dtype)

    @pl.when(g < pl.num_programs(1) - 1)
    def _():
        tail_ref[...] = jnp.dot(inv_ref[1], zb, preferred_element_type=F32)


def _hyout(zh, inv2_bf, x0, s, hbias, hnorm_w, p):
    b, l, c = x0.shape
    nb = l // p
    tok = pl.BlockSpec((1, p, c), lambda i, g: (i, jnp.maximum(g - 1, 0), 0))
    return pl.pallas_call(
        _hyout_kernel,
        grid=(b, nb + 1),
        in_specs=[pl.BlockSpec((1, 1, 2 * p, c), lambda i, g: (i, g, 0, 0)),
                  _const_spec((2, p, 2 * p)),
                  tok, tok, _const_spec((1, c)), _const_spec((1, c))],
        out_specs=tok,
        out_shape=jax.ShapeDtypeStruct((b, l, c), BF16),
        scratch_shapes=[pltpu.VMEM((p, c), F32)],
        compiler_params=_cparams(("parallel", "arbitrary")),
        name="hyout",
    )(zh, inv2_bf, x0, s, hbias.reshape(1, c), hnorm_w.reshape(1, c))


def _attn_kernel(q_ref, k_ref, v_ref, lq1_ref, lk1_ref, lq2_ref, lk2_ref, sw_ref, hs_ref, hk_ref, o_ref, z_ref,
                 qs_ref, m_ref, acc_ref, ss_ref, ks_ref, *, tq, tk, nk, nch, nb, rt, unroll):
    bt, fb = hs_ref.shape[0], hs_ref.shape[3]
    ss_ref[...] = hs_ref[:, :, 0] + hs_ref[:, :, 1]
    ks_ref[...] = hk_ref[:, 0] + hk_ref[:, 1]
    units = (fb // rt) * (nb + 1)
    ntrip = nk // unroll
    per_trip = -(-units // ntrip)

    def conv_unit(idx):
        if isinstance(idx, int):
            tile, e = divmod(min(idx, units - 1), nb + 1)
            rows = pl.ds(tile * rt, rt)
        else:
            idx = jnp.minimum(idx, units - 1)
            tile = lax.div(idx, nb + 1)
            e = lax.rem(idx, nb + 1)
            rows = pl.ds(pl.multiple_of(tile * rt, rt), rt)
        t1 = t2 = t3 = jnp.zeros((bt, rt, LANES), F32)
        for a in range(nb):
            j = e - 1 - a + nb
            t1 = t1 + hk_ref[j, 0, rows, :][None] * hs_ref[:, a, 0, rows, :]
            t2 = t2 + hk_ref[j, 1, rows, :][None] * hs_ref[:, a, 1, rows, :]
            t3 = t3 + ks_ref[j, rows, :][None] * ss_ref[:, a, rows, :]
        z_ref[:, e, 0, rows, :] = t1 - t2
        z_ref[:, e, 1, rows, :] = t3 - t1 - t2

    lane = lax.broadcasted_iota(jnp.int32, (tq, LANES), 1)
    for c in range(nch):
        q = q_ref[0, c * tq:(c + 1) * tq, :]
        zero = jnp.zeros_like(q)
        qs_ref[c, 0:tq, :] = jnp.where(lane < DIFF_HEAD_DIM, q, zero)
        qs_ref[c, tq:2 * tq, :] = jnp.where(lane >= DIFF_HEAD_DIM, q, zero)
    m_ref[...] = jnp.full(m_ref.shape, -jnp.inf, F32)
    acc_ref[...] = jnp.zeros(acc_ref.shape, F32)
    nt = tk // LANES

    def step(i):
        off = i * tk if isinstance(i, int) else pl.multiple_of(i * tk, tk)
        kb = k_ref[0, pl.ds(off, tk), :]
        vb = v_ref[0, pl.ds(off, tk), :]
        for c in range(nch):
            s = lax.dot_general(qs_ref[c], kb, (((1,), (1,)), ((), ())), preferred_element_type=F32)
            st = [s[:, t * LANES:(t + 1) * LANES] for t in range(nt)]
            mt = st[0]
            for t in range(1, nt):
                mt = jnp.maximum(mt, st[t])
            m_prev = m_ref[c]
            m_new = jnp.maximum(m_prev, jnp.max(mt, axis=-1, keepdims=True))
            alpha = jnp.exp2(m_prev - m_new)
            pexp = jnp.concatenate([jnp.exp2(st[t] - m_new).astype(BF16) for t in range(nt)], axis=1)
            pv = jnp.dot(pexp, vb, preferred_element_type=F32)
            acc_ref[c, :, 0:LANES] = alpha * acc_ref[c, :, 0:LANES] + pv[:, 0:LANES]
            acc_ref[c, :, LANES:2 * LANES] = alpha * acc_ref[c, :, LANES:2 * LANES] + pv[:, LANES:2 * LANES]
            m_ref[c] = m_new

    def trip(tr, carry):
        for u in range(per_trip):
            conv_unit(tr * per_trip + u)
        for su in range(unroll):
            step(tr * unroll + su)
        return carry

    if ntrip == 1:
        trip(0, 0)
    else:
        lax.fori_loop(0, ntrip, trip, 0)

    lam = (jnp.exp(jnp.sum(lq1_ref[...] * lk1_ref[...], axis=-1, keepdims=True))
           - jnp.exp(jnp.sum(lq2_ref[...] * lk2_ref[...], axis=-1, keepdims=True)) + LAMBDA_INIT)
    for c in range(nch):
        o = acc_ref[c, :, 0:LANES] / acc_ref[c, :, LANES:2 * LANES]
        o = o[0:tq, :] - lam * o[tq:2 * tq, :]
        ms = jnp.mean(o * o, axis=-1, keepdims=True)
        o = (o * lax.rsqrt(ms + SUBLN_EPS) * sw_ref[...]) * (1.0 - LAMBDA_INIT)
        o_ref[0, c * tq:(c + 1) * tq, :] = o.astype(o_ref.dtype)


def _attn(q, k, v1, lq1, lk1, lq2, lk2, subln_w, sh, kh, p, tq, tk, nch, bt, rt):
    b, l, _ = q.shape
    nk = l // tk
    nqs = l // (nch * tq)
    nb = l // p
    c = HYENA_WIDTH
    nct = c // LANES
    nfg = (b * N_DIFF_HEADS * nqs * bt) // (b * nct)
    fb = p // nfg
    assert fb % rt == 0 and nfg * nct * (b // bt) == b * N_DIFF_HEADS * nqs
    s5 = sh.reshape(b, nb, 2, p, c)
    k4 = kh.reshape(2 * nb, 2, p, c)

    def cmap(i, h, j):
        s = (i * N_DIFF_HEADS + h) * nqs + j
        return s // (nfg * nct), (s // nfg) % nct, s % nfg

    vec = lambda a: a.reshape(1, -1)
    k_spec = pl.BlockSpec((1, l, LANES), lambda i, h, j: (i, 0, h))
    v_spec = pl.BlockSpec((1, l, 2 * LANES), lambda i, h, j: (i, 0, h))
    q_spec = pl.BlockSpec((1, nch * tq, LANES), lambda i, h, j: (i, j, h))
    hs_spec = pl.BlockSpec((bt, nb, 2, fb, LANES), lambda i, h, j: (cmap(i, h, j)[0], 0, 0, cmap(i, h, j)[2], cmap(i, h, j)[1]))
    hk_spec = pl.BlockSpec((2 * nb, 2, fb, LANES), lambda i, h, j: (0, 0, cmap(i, h, j)[2], cmap(i, h, j)[1]))
    z_spec = pl.BlockSpec((bt, nb + 1, 2, fb, LANES), lambda i, h, j: (cmap(i, h, j)[0], 0, 0, cmap(i, h, j)[2], cmap(i, h, j)[1]))
    at, z = pl.pallas_call(
        functools.partial(_attn_kernel, tq=tq, tk=tk, nk=nk, nch=nch, nb=nb, rt=rt, unroll=min(8, nk)),
        grid=(b, N_DIFF_HEADS, nqs),
        in_specs=[q_spec, k_spec, v_spec,
                  _const_spec((1, DIFF_HEAD_DIM)), _const_spec((1, DIFF_HEAD_DIM)),
                  _const_spec((1, DIFF_HEAD_DIM)), _const_spec((1, DIFF_HEAD_DIM)),
                  _const_spec((1, DIFF_V_DIM)), hs_spec, hk_spec],
        out_specs=[q_spec, z_spec],
        out_shape=[jax.ShapeDtypeStruct((b, l, ATTN_WIDTH), BF16),
                   jax.ShapeDtypeStruct((b, nb + 1, 2, p, c), F32)],
        scratch_shapes=[pltpu.VMEM((nch, 2 * tq, LANES), BF16),
                        pltpu.VMEM((nch, 2 * tq, LANES), F32),
                        pltpu.VMEM((nch, 2 * tq, 2 * LANES), F32),
                        pltpu.VMEM((bt, nb, fb, LANES), F32),
                        pltpu.VMEM((2 * nb, fb, LANES), F32)],
        compiler_params=_cparams(("parallel", "parallel", "parallel")),
        name="attn",
    )(q, k, v1, vec(lq1), vec(lk1), vec(lq2), vec(lk2), vec(subln_w), s5, k4)
    return at, z.reshape(b, nb + 1, 2 * p, c)


def _post_kernel(x_ref, hy_ref, at_ref, mod_ref, woh_ref, woa_ref, n2_ref, w1_ref, w2_ref, fw_ref, o_ref, *, ffc):
    x = x_ref[0]
    g1 = mod_ref[0, 2:3, :]
    sh2 = mod_ref[0, 3:4, :]
    sc2 = mod_ref[0, 4:5, :]
    g2 = mod_ref[0, 5:6, :]
    mix = (jnp.dot(hy_ref[0], woh_ref[...], preferred_element_type=F32)
           + jnp.dot(at_ref[0], woa_ref[...], preferred_element_type=F32))
    x1 = x + g1 * mix
    ms = jnp.mean(x1 * x1, axis=-1, keepdims=True)
    hb = ((x1 * lax.rsqrt(ms + NORM_EPS) * n2_ref[...]) * (1.0 + sc2) + sh2).astype(BF16)
    acc = jnp.zeros_like(x1)
    for c0 in range(0, D_FF, ffc):
        hc = jnp.dot(hb, w1_ref[:, c0:c0 + ffc], preferred_element_type=F32)
        hc = jnp.square(jnp.maximum(hc, 0.0))
        acc = acc + jnp.dot(hc.astype(BF16), w2_ref[c0:c0 + ffc, :], preferred_element_type=F32)
    x2 = x1 + g2 * acc
    ms2 = jnp.mean(x2 * x2, axis=-1, keepdims=True)
    o_ref[0] = x2 * lax.rsqrt(ms2 + NORM_EPS) * fw_ref[...]


def _post(x, hy, at, mod, wo_bf, norm2_w, w1_bf, w2_bf, final_w, tm):
    b, l, d = x.shape
    tok = lambda w: pl.BlockSpec((1, tm, w), lambda i, j: (i, j, 0))
    return pl.pallas_call(
        functools.partial(_post_kernel, ffc=1024),
        grid=(b, l // tm),
        in_specs=[tok(d), tok(HYENA_WIDTH), tok(ATTN_WIDTH),
                  pl.BlockSpec((1, 6, d), lambda i, j: (i, 0, 0)),
                  _const_spec((HYENA_WIDTH, d)), _const_spec((ATTN_WIDTH, d)),
                  _const_spec((1, d)), _const_spec((d, D_FF)), _const_spec((D_FF, d)), _const_spec((1, d))],
        out_specs=tok(d),
        out_shape=jax.ShapeDtypeStruct((b, l, d), F32),
        compiler_params=_cparams(("parallel", "parallel")),
        name="post",
    )(x, hy, at, mod, wo_bf[:HYENA_WIDTH], wo_bf[HYENA_WIDTH:], norm2_w.reshape(1, d), w1_bf, w2_bf,
      final_w.reshape(1, d))


def _tiles(b, l):
    tm = min(512, l)
    p = 1024 if l >= 8192 else min(512, l // 2)
    nch = 4
    tq = min(256, l // nch)
    tk = min(512, l)
    nb = l // p
    bt = math.gcd(b, 2)
    rt = (8 * SUBLANES) // bt
    return dict(tm=tm, p=p, tq=tq, tk=tk, nch=nch, bt=bt, rt=rt)


def _encoder(x, mod, w, tables):
    b, l, d = x.shape
    t = _tiles(b, l)
    p = t["p"]
    fwd2_bf, inv2_bf = tables["fwd2"], tables["inv2"]
    cos_t, sin_t = _rope_tables(l)
    sh, x0, s, q, k, v = _inproj(x, mod, w["norm1_w"], w["w_in"], cos_t, sin_t, w["conv_w"], w["conv_b"],
                                 fwd2_bf[0], t["tm"], p)
    kh = _filter_blocks(l, p, w["filt_w1"], w["filt_b1"], w["filt_w2"], w["filt_b2"], w["filt_w3"], w["filt_b3"],
                        w["filt_w4"], w["filt_freq"], fwd2_bf)
    at, zh = _attn(q, k, v, w["lambda_q1"], w["lambda_k1"], w["lambda_q2"], w["lambda_k2"], w["subln_w"],
                   sh, kh, p, t["tq"], t["tk"], t["nch"], t["bt"], t["rt"])
    hy = _hyout(zh, inv2_bf, x0, s, w["hyena_bias"], w["hyena_norm_w"], p)
    return _post(x, hy, at, mod, w["w_out"], w["norm2_w"], w["w_mlp1"], w["w_mlp2"], w["final_w"], t["tm"])


def kernel(x_prompt, x_sample, c_prompt, c_sample, w_ada, b_ada, norm1_w, w_in, conv_w, conv_b, filt_w1, filt_b1, filt_w2, filt_b2, filt_w3, filt_b3, filt_w4, filt_freq, hyena_bias, hyena_norm_w, lambda_q1, lambda_k1, lambda_q2, lambda_k2, subln_w, w_out, norm2_w, w_mlp1, w_mlp2, final_w):
    assert w_ada.shape[0] == 1, "single-layer encoder"
    w = dict(norm1_w=norm1_w[0], w_in=w_in[0].astype(BF16), conv_w=conv_w[0], conv_b=conv_b[0],
             filt_w1=filt_w1[0], filt_b1=filt_b1[0], filt_w2=filt_w2[0], filt_b2=filt_b2[0],
             filt_w3=filt_w3[0], filt_b3=filt_b3[0], filt_w4=filt_w4[0], filt_freq=filt_freq[0],
             hyena_bias=hyena_bias[0], hyena_norm_w=hyena_norm_w[0], lambda_q1=lambda_q1[0],
             lambda_k1=lambda_k1[0], lambda_q2=lambda_q2[0], lambda_k2=lambda_k2[0], subln_w=subln_w[0],
             w_out=w_out[0].astype(BF16), norm2_w=norm2_w[0], w_mlp1=w_mlp1[0].astype(BF16),
             w_mlp2=w_mlp2[0].astype(BF16), final_w=final_w)
    nbp = c_prompt.shape[0]
    mod = _ada(jnp.concatenate([c_prompt, c_sample], axis=0), w_ada[0], b_ada[0])
    mod = mod.reshape(mod.shape[0], 6, D_MODEL)
    outs = []
    for x, m in ((x_prompt, mod[:nbp]), (x_sample, mod[nbp:])):
        p = _tiles(*x.shape[:2])["p"]
        fwd2, inv2 = _dft_mats(p)
        tables = dict(fwd2=fwd2.astype(BF16), inv2=inv2.astype(BF16))
        outs.append(_encoder(x, m, w, tables))
    return tuple(outs)
```

```python
import functools
import math

import numpy as np
import jax
import jax.numpy as jnp
from jax import lax
from jax.experimental import pallas as pl
from jax.experimental.pallas import tpu as pltpu

F32 = jnp.float32
BF16 = jnp.bfloat16

D_MODEL = 1024
HYENA_WIDTH = 512
ATTN_WIDTH = 512
N_DIFF_HEADS = 4
DIFF_HEAD_DIM = 64
DIFF_V_DIM = 2 * DIFF_HEAD_DIM
ROT_DIM = DIFF_HEAD_DIM // 4
ROPE_THETA = 500000.0
D_IN = 3 * HYENA_WIDTH + 3 * ATTN_WIDTH
D_FF = 4 * D_MODEL
FILTER_EMB = 33
FILTER_ORDER = 64
FAST_DECAY_PCT = 0.3
SLOW_DECAY_PCT = 1.5
DECAY_TARGET = 1e-2
NORM_EPS = 1e-6
SUBLN_EPS = 1e-5
LAMBDA_INIT = 0.8 - 0.6 * math.exp(-0.3 * 0)

LANES = 128
SUBLANES = 8
HALO = 2 * SUBLANES
VMEM_LIMIT = 56 * 1024 * 1024

HIGHEST = lax.Precision.HIGHEST


def _cparams(sem):
    return pltpu.CompilerParams(dimension_semantics=sem, vmem_limit_bytes=VMEM_LIMIT)


def _const_spec(shape):
    nd = len(shape)
    return pl.BlockSpec(shape, lambda *_: (0,) * nd, pipeline_mode=pl.Buffered(1))


def _ada_kernel(c_ref, w_ref, b_ref, o_ref):
    c = c_ref[...]
    s = c * jax.nn.sigmoid(c)
    o_ref[...] = jnp.dot(s.astype(BF16), w_ref[...].astype(BF16), preferred_element_type=F32) + b_ref[...]


def _ada(c, w_ada, b_ada):
    nb_, d = c.shape
    n = w_ada.shape[1]
    tn = 1024
    return pl.pallas_call(
        _ada_kernel,
        grid=(n // tn,),
        in_specs=[pl.BlockSpec((nb_, d), lambda j: (0, 0)),
                  pl.BlockSpec((d, tn), lambda j: (0, j)),
                  pl.BlockSpec((1, tn), lambda j: (0, j))],
        out_specs=pl.BlockSpec((nb_, tn), lambda j: (0, j)),
        out_shape=jax.ShapeDtypeStruct((nb_, n), F32),
        compiler_params=_cparams(("parallel",)),
        name="ada",
    )(c, w_ada, b_ada.reshape(1, n))


def _inproj_kernel(x_ref, xp_ref, xn_ref, mod_ref, nw_ref, w_ref, cos_ref, sin_ref, cw_ref, cb_ref, fwd_ref,
                   sh_ref, x0_ref, s_ref, q_ref, k_ref, v_ref, *, nt, tm, r):
    j = pl.program_id(1)
    nsub = x_ref.shape[1] // tm
    c = HYENA_WIDTH
    hw3 = 3 * c
    shift = mod_ref[0, 0:1, :]
    scale = mod_ref[0, 1:2, :]

    def normmod(x):
        ms = jnp.mean(x * x, axis=-1, keepdims=True)
        return ((x * lax.rsqrt(ms + NORM_EPS) * nw_ref[...]) * (1.0 + scale) + shift).astype(BF16)

    hp = jnp.where(j > 0, normmod(xp_ref[0]), jnp.zeros((HALO, x_ref.shape[2]), BF16))
    hn = jnp.where(j < nt - 1, normmod(xn_ref[0]), jnp.zeros((HALO, x_ref.shape[2]), BF16))
    he = jnp.concatenate([hp] + [normmod(x_ref[0, u * tm:(u + 1) * tm, :]) for u in range(nsub)] + [hn], axis=0)

    hl = lax.broadcasted_iota(jnp.int32, (tm, LANES), 1) % DIFF_HEAD_DIM
    lo_half = hl < ROT_DIM // 2
    hi_half = jnp.logical_and(hl >= ROT_DIM // 2, hl < ROT_DIM)
    parts = {}
    for u in range(nsub):
        rows = slice(u * tm, (u + 1) * tm)
        hs = he[u * tm:(u + 1) * tm + 2 * HALO]
        hb = hs[HALO:HALO + tm]
        cvs = []
        for c0 in range(0, hw3, c):
            ue = jnp.dot(hs, w_ref[:, c0:c0 + c], preferred_element_type=F32)
            cvs.append(ue[HALO - 1:HALO - 1 + tm] * cw_ref[0:1, c0:c0 + c] + ue[HALO:HALO + tm] * cw_ref[1:2, c0:c0 + c]
                       + ue[HALO + 1:HALO + 1 + tm] * cw_ref[2:3, c0:c0 + c] + cb_ref[:, c0:c0 + c])
        s = (cvs[1] * cvs[2]).astype(BF16)
        x0_ref[0, rows, :] = cvs[0].astype(BF16)
        s_ref[0, rows, :] = s
        blk, kp = divmod(u, r)
        part = jnp.dot(fwd_ref[:, kp * tm:(kp + 1) * tm], s, preferred_element_type=F32)
        parts[blk] = part if kp == 0 else parts[blk] + part

        cos_t = cos_ref[rows, :]
        sin_t = sin_ref[rows, :]
        sa = jnp.where(lo_half, -sin_t, 0.0)
        sb = jnp.where(hi_half, sin_t, 0.0)

        def rope(xg):
            return (xg * cos_t + pltpu.roll(xg, LANES - ROT_DIM // 2, axis=1) * sa
                    + pltpu.roll(xg, ROT_DIM // 2, axis=1) * sb)

        pq = jnp.dot(hb, w_ref[:, hw3:hw3 + ATTN_WIDTH], preferred_element_type=F32)
        pk = jnp.dot(hb, w_ref[:, hw3 + ATTN_WIDTH:hw3 + 2 * ATTN_WIDTH], preferred_element_type=F32)
        for g in range(ATTN_WIDTH // LANES):
            sl = slice(g * LANES, (g + 1) * LANES)
            q_ref[0, rows, sl] = (rope(pq[:, sl]) * (math.log2(math.e) * DIFF_HEAD_DIM ** -0.5)).astype(BF16)
            k_ref[0, rows, sl] = rope(pk[:, sl]).astype(BF16)
        pv = jnp.dot(hb, w_ref[:, hw3 + 2 * ATTN_WIDTH:], preferred_element_type=F32).astype(BF16)
        ones = jnp.ones((tm, DIFF_V_DIM), BF16)
        for g in range(N_DIFF_HEADS):
            v_ref[0, rows, 2 * g * DIFF_V_DIM:(2 * g + 1) * DIFF_V_DIM] = pv[:, g * DIFF_V_DIM:(g + 1) * DIFF_V_DIM]
            v_ref[0, rows, (2 * g + 1) * DIFF_V_DIM:(2 * g + 2) * DIFF_V_DIM] = ones
    for blk, part in parts.items():
        sh_ref[0, blk] = part


def _inproj(x, mod, norm_w, w_in_bf, cos_t, sin_t, conv_w, conv_b, fwd_bf, tm, nsub, p):
    b, l, d = x.shape
    c = HYENA_WIDTH
    hw3 = 3 * c
    tmb = nsub * tm
    nt = l // tmb
    r = p // tm
    assert nsub % r == 0
    cbs = nsub // r
    rh = tmb // HALO
    nrowh = l // HALO
    tok = lambda w: pl.BlockSpec((1, tmb, w), lambda i, j: (i, j, 0))
    tab = pl.BlockSpec((tmb, LANES), lambda i, j: (j, 0))
    return pl.pallas_call(
        functools.partial(_inproj_kernel, nt=nt, tm=tm, r=r),
        grid=(b, nt),
        in_specs=[tok(d),
                  pl.BlockSpec((1, HALO, d), lambda i, j: (i, jnp.maximum(j * rh - 1, 0), 0)),
                  pl.BlockSpec((1, HALO, d), lambda i, j: (i, jnp.minimum((j + 1) * rh, nrowh - 1), 0)),
                  pl.BlockSpec((1, 6, d), lambda i, j: (i, 0, 0)),
                  _const_spec((1, d)),
                  _const_spec((d, D_IN)),
                  tab, tab,
                  _const_spec((3, hw3)), _const_spec((1, hw3)),
                  _const_spec((2 * p, p))],
        out_specs=[pl.BlockSpec((1, cbs, 2 * p, c), lambda i, j: (i, j, 0, 0)),
                   tok(c), tok(c), tok(ATTN_WIDTH), tok(ATTN_WIDTH), tok(2 * ATTN_WIDTH)],
        out_shape=[jax.ShapeDtypeStruct((b, l // p, 2 * p, c), F32),
                   jax.ShapeDtypeStruct((b, l, c), BF16),
                   jax.ShapeDtypeStruct((b, l, c), BF16),
                   jax.ShapeDtypeStruct((b, l, ATTN_WIDTH), BF16),
                   jax.ShapeDtypeStruct((b, l, ATTN_WIDTH), BF16),
                   jax.ShapeDtypeStruct((b, l, 2 * ATTN_WIDTH), BF16)],
        compiler_params=_cparams(("parallel", "parallel")),
        name="inproj",
    )(x, x, x, mod, norm_w.reshape(1, d), w_in_bf, cos_t, sin_t, conv_w, conv_b.reshape(1, hw3), fwd_bf)


def _rope_tables(l):
    inv_freq = ROPE_THETA ** (-jnp.arange(0, ROT_DIM, 2, dtype=F32) / ROT_DIM)
    hl = np.arange(LANES) % DIFF_HEAD_DIM
    lane_freq = jnp.where(hl < ROT_DIM, inv_freq[hl % (ROT_DIM // 2)], 0.0)
    ang = jnp.arange(l, dtype=F32)[:, None] * lane_freq[None, :]
    return jnp.cos(ang), jnp.sin(ang)


def _dft_mats(p):
    f = np.arange(p, dtype=np.float64)
    n = np.arange(p, dtype=np.float64)
    th = np.pi * (2.0 * f[:, None] + 1.0) * n[None, :] / (2.0 * p)
    fwd = np.concatenate([np.cos(th), -np.sin(th)], axis=0)
    n2 = np.arange(2 * p, dtype=np.float64)
    th2 = np.pi * (2.0 * f[None, :] + 1.0) * n2[:, None] / (2.0 * p)
    inv = np.concatenate([np.cos(th2), -np.sin(th2)], axis=1) / p
    fwd2 = np.stack([fwd, fwd[:, ::-1]], axis=0).astype(np.float32)
    inv2 = np.stack([inv[:p], inv[p:]], axis=0).astype(np.float32)
    return jnp.asarray(fwd2), jnp.asarray(inv2)


def _filt_kernel(fb_ref, w1t_ref, w1c_ref, w1s_ref, b1_ref, w2_ref, b2_ref, w3_ref, b3_ref, w4h_ref, w4b_ref,
                 fr_ref, ad_ref, mat_ref, o_ref, *, l, p, nb):
    j = pl.program_id(0)
    neg = j < nb
    e = jnp.where(neg, nb - 1 - j, j - nb)
    base = e * p + jnp.where(neg, 1, 0)
    posl = (base + lax.broadcasted_iota(jnp.int32, (1, p), 1)).astype(F32)
    tl = posl / float(l - 1)
    wl = (2.0 * math.pi) * posl / float(l)
    ang = fb_ref[...] * wl
    fr = fr_ref[...]
    dot = functools.partial(jnp.dot, precision=HIGHEST, preferred_element_type=F32)
    z1 = w1t_ref[...] * tl + dot(w1c_ref[...], jnp.cos(ang)) - dot(w1s_ref[...], jnp.sin(ang))
    h = jnp.sin(fr * (z1 + b1_ref[...]))
    h = jnp.sin(fr * (dot(w2_ref[...], h) + b2_ref[...]))
    h = jnp.sin(fr * (dot(w3_ref[...], h) + b3_ref[...]))
    row = lax.broadcasted_iota(jnp.int32, (p, 1), 0)
    pos = base + row
    t = pos.astype(F32) / float(l - 1)
    kap = lax.dot_general(h, w4h_ref[...], (((0,), (0,)), ((), ())), precision=HIGHEST,
                          preferred_element_type=F32) * jnp.exp(-t * ad_ref[...])
    kap = jnp.where(pos < l, kap, 0.0)
    hb0 = jnp.sum(h[:, 0:1] * w4b_ref[...], axis=0, keepdims=True)
    kap = kap + jnp.where(jnp.logical_and(row == 0, j == nb), hb0, 0.0)
    o_ref[0] = jnp.dot(mat_ref[0], kap.astype(BF16), preferred_element_type=F32)


def _filter_blocks(l, p, fw1, fb1, fw2, fb2, fw3, fb3, fw4, ffreq, fwd2_bf):
    nb = l // p
    c = HYENA_WIDTH
    bands = (FILTER_EMB - 1) // 2
    fbands = jnp.linspace(1e-4, bands - 1, bands, dtype=F32)[:, None]
    col = lambda a: a.reshape(-1, 1)
    min_decay = math.log(DECAY_TARGET) / SLOW_DECAY_PCT
    max_decay = math.log(DECAY_TARGET) / FAST_DECAY_PCT
    absd = jnp.abs(jnp.linspace(min_decay, max_decay, c, dtype=F32))[None, :]
    half = lambda j: jnp.where(j < nb, 1, 0)
    args = (fbands, col(fw1[0]), fw1[1:1 + bands].T, fw1[1 + bands:].T, col(fb1), fw2.T, col(fb2),
            fw3.T, col(fb3), fw4, fw4[:, c:], col(ffreq), absd, fwd2_bf)
    specs = [_const_spec(a.shape) for a in args]
    specs[9] = pl.BlockSpec((FILTER_ORDER, c), lambda j: (0, half(j)))
    specs[13] = pl.BlockSpec((1, 2 * p, p), lambda j: (half(j), 0, 0))
    return pl.pallas_call(
        functools.partial(_filt_kernel, l=l, p=p, nb=nb),
        grid=(2 * nb,),
        in_specs=specs,
        out_specs=pl.BlockSpec((1, 2 * p, c), lambda j: (j, 0, 0)),
        out_shape=jax.ShapeDtypeStruct((2 * nb, 2 * p, c), F32),
        compiler_params=_cparams(("parallel",)),
        name="filt",
    )(*args)


def _hyout_kernel(z_ref, inv_ref, x0_ref, s_ref, bias_ref, nw_ref, o_ref, tail_ref):
    g = pl.program_id(1)
    zbs = [z_ref[i, 0].astype(BF16) for i in range(z_ref.shape[0])]

    @pl.when(g > 0)
    def _():
        for i, zb in enumerate(zbs):
            y = tail_ref[i] + jnp.dot(inv_ref[0], zb, preferred_element_type=F32)
            y = x0_ref[i].astype(F32) * (y + s_ref[i].astype(F32) * bias_ref[...])
            ms = jnp.mean(y * y, axis=-1, keepdims=True)
            o_ref[i] = (y * lax.rsqrt(ms + NORM_EPS) * nw_ref[...]).astype(o_ref.dtype)

    @pl.when(g < pl.num_programs(1) - 1)
    def _():
        for i, zb in enumerate(zbs):
            tail_ref[i] = jnp.dot(inv_ref[1], zb, preferred_element_type=F32)


def _hyout(zh, inv2_bf, x0, s, hbias, hnorm_w, p):
    b, l, c = x0.shape
    nb = l // p
    bb = math.gcd(b, 2)
    tok = pl.BlockSpec((bb, p, c), lambda i, g: (i, jnp.maximum(g - 1, 0), 0))
    return pl.pallas_call(
        _hyout_kernel,
        grid=(b // bb, nb + 1),
        in_specs=[pl.BlockSpec((bb, 1, 2 * p, c), lambda i, g: (i, g, 0, 0)),
                  _const_spec((2, p, 2 * p)),
                  tok, tok, _const_spec((1, c)), _const_spec((1, c))],
        out_specs=tok,
        out_shape=jax.ShapeDtypeStruct((b, l, c), BF16),
        scratch_shapes=[pltpu.VMEM((bb, p, c), F32)],
        compiler_params=_cparams(("parallel", "arbitrary")),
        name="hyout",
    )(zh, inv2_bf, x0, s, hbias.reshape(1, c), hnorm_w.reshape(1, c))


def _attn_kernel(q_ref, k_ref, v_ref, lq1_ref, lk1_ref, lq2_ref, lk2_ref, sw_ref, hs_ref, hk_ref, o_ref, z_ref,
                 qs_ref, m_ref, acc_ref, ss_ref, ks_ref, *, tq, tk, nk, nch, nb, rt, unroll):
    bt, fb = hs_ref.shape[0], hs_ref.shape[3]
    ss_ref[...] = hs_ref[:, :, 0] + hs_ref[:, :, 1]
    ks_ref[...] = hk_ref[:, 0] + hk_ref[:, 1]
    units = (fb // rt) * (nb + 1)
    ntrip = nk // unroll
    per_trip = -(-units // ntrip)

    def conv_unit(idx):
        if isinstance(idx, int):
            tile, e = divmod(min(idx, units - 1), nb + 1)
            rows = pl.ds(tile * rt, rt)
        else:
            idx = jnp.minimum(idx, units - 1)
            tile = lax.div(idx, nb + 1)
            e = lax.rem(idx, nb + 1)
            rows = pl.ds(pl.multiple_of(tile * rt, rt), rt)
        t1 = t2 = t3 = jnp.zeros((bt, rt, LANES), F32)
        for a in range(nb):
            j = e - 1 - a + nb
            t1 = t1 + hk_ref[j, 0, rows, :][None] * hs_ref[:, a, 0, rows, :]
            t2 = t2 + hk_ref[j, 1, rows, :][None] * hs_ref[:, a, 1, rows, :]
            t3 = t3 + ks_ref[j, rows, :][None] * ss_ref[:, a, rows, :]
        z_ref[:, e, 0, rows, :] = t1 - t2
        z_ref[:, e, 1, rows, :] = t3 - t1 - t2

    lane = lax.broadcasted_iota(jnp.int32, (tq, LANES), 1)
    for c in range(nch):
        q = q_ref[0, c * tq:(c + 1) * tq, :]
        zero = jnp.zeros_like(q)
        qs_ref[c, 0:tq, :] = jnp.where(lane < DIFF_HEAD_DIM, q, zero)
        qs_ref[c, tq:2 * tq, :] = jnp.where(lane >= DIFF_HEAD_DIM, q, zero)
    m_ref[...] = jnp.full(m_ref.shape, -jnp.inf, F32)
    acc_ref[...] = jnp.zeros(acc_ref.shape, F32)
    nt = tk // LANES

    def step(i):
        off = i * tk if isinstance(i, int) else pl.multiple_of(i * tk, tk)
        kb = k_ref[0, pl.ds(off, tk), :]
        vb = v_ref[0, pl.ds(off, tk), :]
        for c in range(nch):
            s = lax.dot_general(qs_ref[c], kb, (((1,), (1,)), ((), ())), preferred_element_type=F32)
            st = [s[:, t * LANES:(t + 1) * LANES] for t in range(nt)]
            mt = st[0]
            for t in range(1, nt):
                mt = jnp.maximum(mt, st[t])
            m_prev = m_ref[c]
            m_new = jnp.maximum(m_prev, jnp.max(mt, axis=-1, keepdims=True))
            alpha = jnp.exp2(m_prev - m_new)
            pexp = jnp.concatenate([jnp.exp2(st[t] - m_new).astype(BF16) for t in range(nt)], axis=1)
            pv = jnp.dot(pexp, vb, preferred_element_type=F32)
            acc_ref[c, :, 0:LANES] = alpha * acc_ref[c, :, 0:LANES] + pv[:, 0:LANES]
            acc_ref[c, :, LANES:2 * LANES] = alpha * acc_ref[c, :, LANES:2 * LANES] + pv[:, LANES:2 * LANES]
            m_ref[c] = m_new

    def trip(tr, carry):
        for u in range(per_trip):
            conv_unit(tr * per_trip + u)
        for su in range(unroll):
            step(tr * unroll + su)
        return carry

    if ntrip == 1:
        trip(0, 0)
    else:
        lax.fori_loop(0, ntrip, trip, 0)

    lam = (jnp.exp(jnp.sum(lq1_ref[...] * lk1_ref[...], axis=-1, keepdims=True))
           - jnp.exp(jnp.sum(lq2_ref[...] * lk2_ref[...], axis=-1, keepdims=True)) + LAMBDA_INIT)
    for c in range(nch):
        o = acc_ref[c, :, 0:LANES] / acc_ref[c, :, LANES:2 * LANES]
        o = o[0:tq, :] - lam * o[tq:2 * tq, :]
        ms = jnp.mean(o * o, axis=-1, keepdims=True)
        o = (o * lax.rsqrt(ms + SUBLN_EPS) * sw_ref[...]) * (1.0 - LAMBDA_INIT)
        o_ref[0, c * tq:(c + 1) * tq, :] = o.astype(o_ref.dtype)


def _attn(q, k, v1, lq1, lk1, lq2, lk2, subln_w, sh, kh, p, tq, tk, nch, bt, rt):
    b, l, _ = q.shape
    nk = l // tk
    nqs = l // (nch * tq)
    nb = l // p
    c = HYENA_WIDTH
    nct = c // LANES
    nfg = (b * N_DIFF_HEADS * nqs * bt) // (b * nct)
    fb = p // nfg
    assert fb % rt == 0 and nfg * nct * (b // bt) == b * N_DIFF_HEADS * nqs
    s5 = sh.reshape(b, nb, 2, p, c)
    k4 = kh.reshape(2 * nb, 2, p, c)

    def cmap(i, h, j):
        s = (i * N_DIFF_HEADS + h) * nqs + j
        return s // (nfg * nct), (s // nfg) % nct, s % nfg

    vec = lambda a: a.reshape(1, -1)
    k_spec = pl.BlockSpec((1, l, LANES), lambda i, h, j: (i, 0, h))
    v_spec = pl.BlockSpec((1, l, 2 * LANES), lambda i, h, j: (i, 0, h))
    q_spec = pl.BlockSpec((1, nch * tq, LANES), lambda i, h, j: (i, j, h))
    hs_spec = pl.BlockSpec((bt, nb, 2, fb, LANES), lambda i, h, j: (cmap(i, h, j)[0], 0, 0, cmap(i, h, j)[2], cmap(i, h, j)[1]))
    hk_spec = pl.BlockSpec((2 * nb, 2, fb, LANES), lambda i, h, j: (0, 0, cmap(i, h, j)[2], cmap(i, h, j)[1]))
    z_spec = pl.BlockSpec((bt, nb + 1, 2, fb, LANES), lambda i, h, j: (cmap(i, h, j)[0], 0, 0, cmap(i, h, j)[2], cmap(i, h, j)[1]))
    at, z = pl.pallas_call(
        functools.partial(_attn_kernel, tq=tq, tk=tk, nk=nk, nch=nch, nb=nb, rt=rt, unroll=min(8, nk)),
        grid=(b, N_DIFF_HEADS, nqs),
        in_specs=[q_spec, k_spec, v_spec,
                  _const_spec((1, DIFF_HEAD_DIM)), _const_spec((1, DIFF_HEAD_DIM)),
                  _const_spec((1, DIFF_HEAD_DIM)), _const_spec((1, DIFF_HEAD_DIM)),
                  _const_spec((1, DIFF_V_DIM)), hs_spec, hk_spec],
        out_specs=[q_spec, z_spec],
        out_shape=[jax.ShapeDtypeStruct((b, l, ATTN_WIDTH), BF16),
                   jax.ShapeDtypeStruct((b, nb + 1, 2, p, c), F32)],
        scratch_shapes=[pltpu.VMEM((nch, 2 * tq, LANES), BF16),
                        pltpu.VMEM((nch, 2 * tq, LANES), F32),
                        pltpu.VMEM((nch, 2 * tq, 2 * LANES), F32),
                        pltpu.VMEM((bt, nb, fb, LANES), F32),
                        pltpu.VMEM((2 * nb, fb, LANES), F32)],
        compiler_params=_cparams(("parallel", "parallel", "parallel")),
        name="attn",
    )(q, k, v1, vec(lq1), vec(lk1), vec(lq2), vec(lk2), vec(subln_w), s5, k4)
    return at, z.reshape(b, nb + 1, 2 * p, c)


def _post_kernel(x_ref, hy_ref, at_ref, mod_ref, woh_ref, woa_ref, n2_ref, w1_ref, w2_ref, fw_ref, o_ref, *, ffc):
    x = x_ref[0]
    g1 = mod_ref[0, 2:3, :]
    sh2 = mod_ref[0, 3:4, :]
    sc2 = mod_ref[0, 4:5, :]
    g2 = mod_ref[0, 5:6, :]
    mix = (jnp.dot(hy_ref[0], woh_ref[...], preferred_element_type=F32)
           + jnp.dot(at_ref[0], woa_ref[...], preferred_element_type=F32))
    x1 = x + g1 * mix
    ms = jnp.mean(x1 * x1, axis=-1, keepdims=True)
    hb = ((x1 * lax.rsqrt(ms + NORM_EPS) * n2_ref[...]) * (1.0 + sc2) + sh2).astype(BF16)
    acc = jnp.zeros_like(x1)
    for c0 in range(0, D_FF, ffc):
        hc = jnp.dot(hb, w1_ref[:, c0:c0 + ffc], preferred_element_type=F32)
        hc = jnp.square(jnp.maximum(hc, 0.0))
        acc = acc + jnp.dot(hc.astype(BF16), w2_ref[c0:c0 + ffc, :], preferred_element_type=F32)
    x2 = x1 + g2 * acc
    ms2 = jnp.mean(x2 * x2, axis=-1, keepdims=True)
    o_ref[0] = x2 * lax.rsqrt(ms2 + NORM_EPS) * fw_ref[...]


def _post(x, hy, at, mod, wo_bf, norm2_w, w1_bf, w2_bf, final_w, tm):
    b, l, d = x.shape
    tok = lambda w: pl.BlockSpec((1, tm, w), lambda i, j: (i, j, 0))
    return pl.pallas_call(
        functools.partial(_post_kernel, ffc=1024),
        grid=(b, l // tm),
        in_specs=[tok(d), tok(HYENA_WIDTH), tok(ATTN_WIDTH),
                  pl.BlockSpec((1, 6, d), lambda i, j: (i, 0, 0)),
                  _const_spec((HYENA_WIDTH, d)), _const_spec((ATTN_WIDTH, d)),
                  _const_spec((1, d)), _const_spec((d, D_FF)), _const_spec((D_FF, d)), _const_spec((1, d))],
        out_specs=tok(d),
        out_shape=jax.ShapeDtypeStruct((b, l, d), F32),
        compiler_params=_cparams(("parallel", "parallel")),
        name="post",
    )(x, hy, at, mod, wo_bf[:HYENA_WIDTH], wo_bf[HYENA_WIDTH:], norm2_w.reshape(1, d), w1_bf, w2_bf,
      final_w.reshape(1, d))


def _tiles(b, l):
    tm = min(512, l)
    nsub = 2 if l >= 2 * tm else 1
    p = 1024 if l >= 8192 else min(512, l // 2)
    nch = 4
    tq = min(256, l // nch)
    tk = min(512, l)
    bt = math.gcd(b, 2)
    rt = (8 * SUBLANES) // bt
    return dict(tm=tm, nsub=nsub, p=p, tq=tq, tk=tk, nch=nch, bt=bt, rt=rt)


def _encoder(x, mod, w, tables):
    b, l, d = x.shape
    t = _tiles(b, l)
    p = t["p"]
    fwd2_bf, inv2_bf = tables["fwd2"], tables["inv2"]
    cos_t, sin_t = _rope_tables(l)
    sh, x0, s, q, k, v = _inproj(x, mod, w["norm1_w"], w["w_in"], cos_t, sin_t, w["conv_w"], w["conv_b"],
                                 fwd2_bf[0], t["tm"], t["nsub"], p)
    kh = _filter_blocks(l, p, w["filt_w1"], w["filt_b1"], w["filt_w2"], w["filt_b2"], w["filt_w3"], w["filt_b3"],
                        w["filt_w4"], w["filt_freq"], fwd2_bf)
    at, zh = _attn(q, k, v, w["lambda_q1"], w["lambda_k1"], w["lambda_q2"], w["lambda_k2"], w["subln_w"],
                   sh, kh, p, t["tq"], t["tk"], t["nch"], t["bt"], t["rt"])
    hy = _hyout(zh, inv2_bf, x0, s, w["hyena_bias"], w["hyena_norm_w"], p)
    return _post(x, hy, at, mod, w["w_out"], w["norm2_w"], w["w_mlp1"], w["w_mlp2"], w["final_w"], t["tm"])


def kernel(x_prompt, x_sample, c_prompt, c_sample, w_ada, b_ada, norm1_w, w_in, conv_w, conv_b, filt_w1, filt_b1, filt_w2, filt_b2, filt_w3, filt_b3, filt_w4, filt_freq, hyena_bias, hyena_norm_w, lambda_q1, lambda_k1, lambda_q2, lambda_k2, subln_w, w_out, norm2_w, w_mlp1, w_mlp2, final_w):
    assert w_ada.shape[0] == 1, "single-layer encoder"
    w = dict(norm1_w=norm1_w[0], w_in=w_in[0].astype(BF16), conv_w=conv_w[0], conv_b=conv_b[0],
             filt_w1=filt_w1[0], filt_b1=filt_b1[0], filt_w2=filt_w2[0], filt_b2=filt_b2[0],
             filt_w3=filt_w3[0], filt_b3=filt_b3[0], filt_w4=filt_w4[0], filt_freq=filt_freq[0],
             hyena_bias=hyena_bias[0], hyena_norm_w=hyena_norm_w[0], lambda_q1=lambda_q1[0],
             lambda_k1=lambda_k1[0], lambda_q2=lambda_q2[0], lambda_k2=lambda_k2[0], subln_w=subln_w[0],
             w_out=w_out[0].astype(BF16), norm2_w=norm2_w[0], w_mlp1=w_mlp1[0].astype(BF16),
             w_mlp2=w_mlp2[0].astype(BF16), final_w=final_w)
    nbp = c_prompt.shape[0]
    mod = _ada(jnp.concatenate([c_prompt, c_sample], axis=0), w_ada[0], b_ada[0])
    mod = mod.reshape(mod.shape[0], 6, D_MODEL)
    outs = []
    for x, m in ((x_prompt, mod[:nbp]), (x_sample, mod[nbp:])):
        p = _tiles(*x.shape[:2])["p"]
        fwd2, inv2 = _dft_mats(p)
        tables = dict(fwd2=fwd2.astype(BF16), inv2=inv2.astype(BF16))
        outs.append(_encoder(x, m, w, tables))
    return tuple(outs)
```

```python
import functools
import math

import numpy as np
import jax
import jax.numpy as jnp
from jax import lax
from jax.experimental import pallas as pl
from jax.experimental.pallas import tpu as pltpu

F32 = jnp.float32
BF16 = jnp.bfloat16

D_MODEL = 1024
HYENA_WIDTH = 512
ATTN_WIDTH = 512
N_DIFF_HEADS = 4
DIFF_HEAD_DIM = 64
DIFF_V_DIM = 2 * DIFF_HEAD_DIM
ROT_DIM = DIFF_HEAD_DIM // 4
ROPE_THETA = 500000.0
D_IN = 3 * HYENA_WIDTH + 3 * ATTN_WIDTH
D_FF = 4 * D_MODEL
FILTER_EMB = 33
FILTER_ORDER = 64
FAST_DECAY_PCT = 0.3
SLOW_DECAY_PCT = 1.5
DECAY_TARGET = 1e-2
NORM_EPS = 1e-6
SUBLN_EPS = 1e-5
LAMBDA_INIT = 0.8 - 0.6 * math.exp(-0.3 * 0)

LANES = 128
SUBLANES = 8
HALO = 2 * SUBLANES
VMEM_LIMIT = 56 * 1024 * 1024

HIGHEST = lax.Precision.HIGHEST


def _cparams(sem):
    return pltpu.CompilerParams(dimension_semantics=sem, vmem_limit_bytes=VMEM_LIMIT)


def _const_spec(shape):
    nd = len(shape)
    return pl.BlockSpec(shape, lambda *_: (0,) * nd, pipeline_mode=pl.Buffered(1))


def _ada_kernel(c_ref, w_ref, b_ref, o_ref):
    c = c_ref[...]
    s = c * jax.nn.sigmoid(c)
    o_ref[...] = jnp.dot(s.astype(BF16), w_ref[...].astype(BF16), preferred_element_type=F32) + b_ref[...]


def _ada(c, w_ada, b_ada):
    nb_, d = c.shape
    n = w_ada.shape[1]
    tn = 1024
    return pl.pallas_call(
        _ada_kernel,
        grid=(n // tn,),
        in_specs=[pl.BlockSpec((nb_, d), lambda j: (0, 0)),
                  pl.BlockSpec((d, tn), lambda j: (0, j)),
                  pl.BlockSpec((1, tn), lambda j: (0, j))],
        out_specs=pl.BlockSpec((nb_, tn), lambda j: (0, j)),
        out_shape=jax.ShapeDtypeStruct((nb_, n), F32),
        compiler_params=_cparams(("parallel",)),
        name="ada",
    )(c, w_ada, b_ada.reshape(1, n))


def _inproj_kernel(x_ref, xp_ref, xn_ref, mod_ref, nw_ref, w_ref, cos_ref, sin_ref, cw_ref, cb_ref, fwd_ref,
                   sh_ref, x0_ref, s_ref, q_ref, k_ref, v_ref, *, nt, tm, r):
    j = pl.program_id(1)
    nsub = x_ref.shape[1] // tm
    c = HYENA_WIDTH
    hw3 = 3 * c
    shift = mod_ref[0, 0:1, :]
    scale = mod_ref[0, 1:2, :]

    def normmod(x):
        ms = jnp.mean(x * x, axis=-1, keepdims=True)
        return ((x * lax.rsqrt(ms + NORM_EPS) * nw_ref[...]) * (1.0 + scale) + shift).astype(BF16)

    hp = jnp.where(j > 0, normmod(xp_ref[0]), jnp.zeros((HALO, x_ref.shape[2]), BF16))
    hn = jnp.where(j < nt - 1, normmod(xn_ref[0]), jnp.zeros((HALO, x_ref.shape[2]), BF16))
    he = jnp.concatenate([hp] + [normmod(x_ref[0, u * tm:(u + 1) * tm, :]) for u in range(nsub)] + [hn], axis=0)

    hl = lax.broadcasted_iota(jnp.int32, (tm, LANES), 1) % DIFF_HEAD_DIM
    lo_half = hl < ROT_DIM // 2
    hi_half = jnp.logical_and(hl >= ROT_DIM // 2, hl < ROT_DIM)
    parts = {}
    for u in range(nsub):
        rows = slice(u * tm, (u + 1) * tm)
        hs = he[u * tm:(u + 1) * tm + 2 * HALO]
        hb = hs[HALO:HALO + tm]
        cvs = []
        for c0 in range(0, hw3, c):
            ue = jnp.dot(hs, w_ref[:, c0:c0 + c], preferred_element_type=F32)
            cvs.append(ue[HALO - 1:HALO - 1 + tm] * cw_ref[0:1, c0:c0 + c] + ue[HALO:HALO + tm] * cw_ref[1:2, c0:c0 + c]
                       + ue[HALO + 1:HALO + 1 + tm] * cw_ref[2:3, c0:c0 + c] + cb_ref[:, c0:c0 + c])
        s = (cvs[1] * cvs[2]).astype(BF16)
        x0_ref[0, rows, :] = cvs[0].astype(BF16)
        s_ref[0, rows, :] = s
        blk, kp = divmod(u, r)
        part = jnp.dot(fwd_ref[:, kp * tm:(kp + 1) * tm], s, preferred_element_type=F32)
        parts[blk] = part if kp == 0 else parts[blk] + part

        cos_t = cos_ref[rows, :]
        sin_t = sin_ref[rows, :]
        sa = jnp.where(lo_half, -sin_t, 0.0)
        sb = jnp.where(hi_half, sin_t, 0.0)

        def rope(xg):
            return (xg * cos_t + pltpu.roll(xg, LANES - ROT_DIM // 2, axis=1) * sa
                    + pltpu.roll(xg, ROT_DIM // 2, axis=1) * sb)

        pq = jnp.dot(hb, w_ref[:, hw3:hw3 + ATTN_WIDTH], preferred_element_type=F32)
        pk = jnp.dot(hb, w_ref[:, hw3 + ATTN_WIDTH:hw3 + 2 * ATTN_WIDTH], preferred_element_type=F32)
        for g in range(ATTN_WIDTH // LANES):
            sl = slice(g * LANES, (g + 1) * LANES)
            q_ref[0, rows, sl] = (rope(pq[:, sl]) * (math.log2(math.e) * DIFF_HEAD_DIM ** -0.5)).astype(BF16)
            k_ref[0, rows, sl] = rope(pk[:, sl]).astype(BF16)
        pv = jnp.dot(hb, w_ref[:, hw3 + 2 * ATTN_WIDTH:], preferred_element_type=F32).astype(BF16)
        ones = jnp.ones((tm, DIFF_V_DIM), BF16)
        for g in range(N_DIFF_HEADS):
            v_ref[0, rows, 2 * g * DIFF_V_DIM:(2 * g + 1) * DIFF_V_DIM] = pv[:, g * DIFF_V_DIM:(g + 1) * DIFF_V_DIM]
            v_ref[0, rows, (2 * g + 1) * DIFF_V_DIM:(2 * g + 2) * DIFF_V_DIM] = ones
    for blk, part in parts.items():
        sh_ref[0, blk] = part


def _inproj(x, mod, norm_w, w_in_bf, cos_t, sin_t, conv_w, conv_b, fwd_bf, tm, nsub, p):
    b, l, d = x.shape
    c = HYENA_WIDTH
    hw3 = 3 * c
    tmb = nsub * tm
    nt = l // tmb
    r = p // tm
    assert nsub % r == 0
    cbs = nsub // r
    rh = tmb // HALO
    nrowh = l // HALO
    tok = lambda w: pl.BlockSpec((1, tmb, w), lambda i, j: (i, j, 0))
    tab = pl.BlockSpec((tmb, LANES), lambda i, j: (j, 0))
    return pl.pallas_call(
        functools.partial(_inproj_kernel, nt=nt, tm=tm, r=r),
        grid=(b, nt),
        in_specs=[tok(d),
                  pl.BlockSpec((1, HALO, d), lambda i, j: (i, jnp.maximum(j * rh - 1, 0), 0)),
                  pl.BlockSpec((1, HALO, d), lambda i, j: (i, jnp.minimum((j + 1) * rh, nrowh - 1), 0)),
                  pl.BlockSpec((1, 6, d), lambda i, j: (i, 0, 0)),
                  _const_spec((1, d)),
                  _const_spec((d, D_IN)),
                  tab, tab,
                  _const_spec((3, hw3)), _const_spec((1, hw3)),
                  _const_spec((2 * p, p))],
        out_specs=[pl.BlockSpec((1, cbs, 2 * p, c), lambda i, j: (i, j, 0, 0)),
                   tok(c), tok(c), tok(ATTN_WIDTH), tok(ATTN_WIDTH), tok(2 * ATTN_WIDTH)],
        out_shape=[jax.ShapeDtypeStruct((b, l // p, 2 * p, c), F32),
                   jax.ShapeDtypeStruct((b, l, c), BF16),
                   jax.ShapeDtypeStruct((b, l, c), BF16),
                   jax.ShapeDtypeStruct((b, l, ATTN_WIDTH), BF16),
                   jax.ShapeDtypeStruct((b, l, ATTN_WIDTH), BF16),
                   jax.ShapeDtypeStruct((b, l, 2 * ATTN_WIDTH), BF16)],
        compiler_params=_cparams(("parallel", "parallel")),
        name="inproj",
    )(x, x, x, mod, norm_w.reshape(1, d), w_in_bf, cos_t, sin_t, conv_w, conv_b.reshape(1, hw3), fwd_bf)


def _rope_tables(l):
    inv_freq = ROPE_THETA ** (-jnp.arange(0, ROT_DIM, 2, dtype=F32) / ROT_DIM)
    hl = np.arange(LANES) % DIFF_HEAD_DIM
    lane_freq = jnp.where(hl < ROT_DIM, inv_freq[hl % (ROT_DIM // 2)], 0.0)
    ang = jnp.arange(l, dtype=F32)[:, None] * lane_freq[None, :]
    return jnp.cos(ang), jnp.sin(ang)


def _dft_mats(p):
    f = np.arange(p, dtype=np.float64)
    n = np.arange(p, dtype=np.float64)
    th = np.pi * (2.0 * f[:, None] + 1.0) * n[None, :] / (2.0 * p)
    fwd = np.concatenate([np.cos(th), -np.sin(th)], axis=0)
    n2 = np.arange(2 * p, dtype=np.float64)
    th2 = np.pi * (2.0 * f[None, :] + 1.0) * n2[:, None] / (2.0 * p)
    inv = np.concatenate([np.cos(th2), -np.sin(th2)], axis=1) / p
    fwd2 = np.stack([fwd, fwd[:, ::-1]], axis=0).astype(np.float32)
    inv2 = np.stack([inv[:p], inv[p:]], axis=0).astype(np.float32)
    return jnp.asarray(fwd2), jnp.asarray(inv2)


def _filt_kernel(fb_ref, w1t_ref, w1c_ref, w1s_ref, b1_ref, w2_ref, b2_ref, w3_ref, b3_ref, w4h_ref, w4b_ref,
                 fr_ref, ad_ref, mat_ref, o_ref, *, l, p, nb):
    j = pl.program_id(0)
    neg = j < nb
    e = jnp.where(neg, nb - 1 - j, j - nb)
    base = e * p + jnp.where(neg, 1, 0)
    posl = (base + lax.broadcasted_iota(jnp.int32, (1, p), 1)).astype(F32)
    tl = posl / float(l - 1)
    wl = (2.0 * math.pi) * posl / float(l)
    ang = fb_ref[...] * wl
    fr = fr_ref[...]
    dot = functools.partial(jnp.dot, precision=HIGHEST, preferred_element_type=F32)
    z1 = w1t_ref[...] * tl + dot(w1c_ref[...], jnp.cos(ang)) - dot(w1s_ref[...], jnp.sin(ang))
    h = jnp.sin(fr * (z1 + b1_ref[...]))
    h = jnp.sin(fr * (dot(w2_ref[...], h) + b2_ref[...]))
    h = jnp.sin(fr * (dot(w3_ref[...], h) + b3_ref[...]))
    row = lax.broadcasted_iota(jnp.int32, (p, 1), 0)
    pos = base + row
    t = pos.astype(F32) / float(l - 1)
    kap = lax.dot_general(h, w4h_ref[...], (((0,), (0,)), ((), ())), precision=HIGHEST,
                          preferred_element_type=F32) * jnp.exp(-t * ad_ref[...])
    kap = jnp.where(pos < l, kap, 0.0)
    hb0 = jnp.sum(h[:, 0:1] * w4b_ref[...], axis=0, keepdims=True)
    kap = kap + jnp.where(jnp.logical_and(row == 0, j == nb), hb0, 0.0)
    o_ref[0] = jnp.dot(mat_ref[0], kap.astype(BF16), preferred_element_type=F32)


def _filter_blocks(l, p, fw1, fb1, fw2, fb2, fw3, fb3, fw4, ffreq, fwd2_bf):
    nb = l // p
    c = HYENA_WIDTH
    bands = (FILTER_EMB - 1) // 2
    fbands = jnp.linspace(1e-4, bands - 1, bands, dtype=F32)[:, None]
    col = lambda a: a.reshape(-1, 1)
    min_decay = math.log(DECAY_TARGET) / SLOW_DECAY_PCT
    max_decay = math.log(DECAY_TARGET) / FAST_DECAY_PCT
    absd = jnp.abs(jnp.linspace(min_decay, max_decay, c, dtype=F32))[None, :]
    half = lambda j: jnp.where(j < nb, 1, 0)
    args = (fbands, col(fw1[0]), fw1[1:1 + bands].T, fw1[1 + bands:].T, col(fb1), fw2.T, col(fb2),
            fw3.T, col(fb3), fw4, fw4[:, c:], col(ffreq), absd, fwd2_bf)
    specs = [_const_spec(a.shape) for a in args]
    specs[9] = pl.BlockSpec((FILTER_ORDER, c), lambda j: (0, half(j)))
    specs[13] = pl.BlockSpec((1, 2 * p, p), lambda j: (half(j), 0, 0))
    return pl.pallas_call(
        functools.partial(_filt_kernel, l=l, p=p, nb=nb),
        grid=(2 * nb,),
        in_specs=specs,
        out_specs=pl.BlockSpec((1, 2 * p, c), lambda j: (j, 0, 0)),
        out_shape=jax.ShapeDtypeStruct((2 * nb, 2 * p, c), F32),
        compiler_params=_cparams(("parallel",)),
        name="filt",
    )(*args)


def _hyout_kernel(z_ref, inv_ref, x0_ref, s_ref, bias_ref, nw_ref, o_ref, tail_ref):
    g = pl.program_id(1)
    zbs = [z_ref[i, 0].astype(BF16) for i in range(z_ref.shape[0])]

    @pl.when(g > 0)
    def _():
        for i, zb in enumerate(zbs):
            y = tail_ref[i] + jnp.dot(inv_ref[0], zb, preferred_element_type=F32)
            y = x0_ref[i].astype(F32) * (y + s_ref[i].astype(F32) * bias_ref[...])
            ms = jnp.mean(y * y, axis=-1, keepdims=True)
            o_ref[i] = (y * lax.rsqrt(ms + NORM_EPS) * nw_ref[...]).astype(o_ref.dtype)

    @pl.when(g < pl.num_programs(1) - 1)
    def _():
        for i, zb in enumerate(zbs):
            tail_ref[i] = jnp.dot(inv_ref[1], zb, preferred_element_type=F32)


def _hyout(zh, inv2_bf, x0, s, hbias, hnorm_w, p):
    b, l, c = x0.shape
    nb = l // p
    bb = math.gcd(b, 2)
    tok = pl.BlockSpec((bb, p, c), lambda i, g: (i, jnp.maximum(g - 1, 0), 0))
    return pl.pallas_call(
        _hyout_kernel,
        grid=(b // bb, nb + 1),
        in_specs=[pl.BlockSpec((bb, 1, 2 * p, c), lambda i, g: (i, g, 0, 0)),
                  _const_spec((2, p, 2 * p)),
                  tok, tok, _const_spec((1, c)), _const_spec((1, c))],
        out_specs=tok,
        out_shape=jax.ShapeDtypeStruct((b, l, c), BF16),
        scratch_shapes=[pltpu.VMEM((bb, p, c), F32)],
        compiler_params=_cparams(("parallel", "arbitrary")),
        name="hyout",
    )(zh, inv2_bf, x0, s, hbias.reshape(1, c), hnorm_w.reshape(1, c))


def _attn_kernel(q_ref, k_ref, v_ref, lq1_ref, lk1_ref, lq2_ref, lk2_ref, sw_ref, hs_ref, hk_ref, o_ref, z_ref,
                 qs_ref, m_ref, acc_ref, ss_ref, ks_ref, *, tq, tk, nk, nch, nb, rt, unroll):
    bt, fb = hs_ref.shape[0], hs_ref.shape[3]
    ss_ref[...] = hs_ref[:, :, 0] + hs_ref[:, :, 1]
    ks_ref[...] = hk_ref[:, 0] + hk_ref[:, 1]
    units = (fb // rt) * (nb + 1)
    ntrip = nk // unroll
    per_trip = -(-units // ntrip)

    def conv_unit(idx):
        if isinstance(idx, int):
            tile, e = divmod(min(idx, units - 1), nb + 1)
            rows = pl.ds(tile * rt, rt)
        else:
            idx = jnp.minimum(idx, units - 1)
            tile = lax.div(idx, nb + 1)
            e = lax.rem(idx, nb + 1)
            rows = pl.ds(pl.multiple_of(tile * rt, rt), rt)
        t1 = t2 = t3 = jnp.zeros((bt, rt, LANES), F32)
        for a in range(nb):
            j = e - 1 - a + nb
            t1 = t1 + hk_ref[j, 0, rows, :][None] * hs_ref[:, a, 0, rows, :]
            t2 = t2 + hk_ref[j, 1, rows, :][None] * hs_ref[:, a, 1, rows, :]
            t3 = t3 + ks_ref[j, rows, :][None] * ss_ref[:, a, rows, :]
        z_ref[:, e, 0, rows, :] = t1 - t2
        z_ref[:, e, 1, rows, :] = t3 - t1 - t2

    lane = lax.broadcasted_iota(jnp.int32, (tq, LANES), 1)
    for c in range(nch):
        q = q_ref[0, c * tq:(c + 1) * tq, :]
        zero = jnp.zeros_like(q)
        qs_ref[c, 0:tq, :] = jnp.where(lane < DIFF_HEAD_DIM, q, zero)
        qs_ref[c, tq:2 * tq, :] = jnp.where(lane >= DIFF_HEAD_DIM, q, zero)
    m_ref[...] = jnp.full(m_ref.shape, -jnp.inf, F32)
    acc_ref[...] = jnp.zeros(acc_ref.shape, F32)
    nt = tk // LANES

    def step(i):
        off = i * tk if isinstance(i, int) else pl.multiple_of(i * tk, tk)
        kb = k_ref[0, pl.ds(off, tk), :]
        vb = v_ref[0, pl.ds(off, tk), :]
        for c in range(nch):
            s = lax.dot_general(qs_ref[c], kb, (((1,), (1,)), ((), ())), preferred_element_type=F32)
            st = [s[:, t * LANES:(t + 1) * LANES] for t in range(nt)]
            mt = st[0]
            for t in range(1, nt):
                mt = jnp.maximum(mt, st[t])
            m_prev = m_ref[c]
            m_new = jnp.maximum(m_prev, jnp.max(mt, axis=-1, keepdims=True))
            alpha = jnp.exp2(m_prev - m_new)
            pexp = jnp.concatenate([jnp.exp2(st[t] - m_new).astype(BF16) for t in range(nt)], axis=1)
            pv = jnp.dot(pexp, vb, preferred_element_type=F32)
            acc_ref[c, :, 0:LANES] = alpha * acc_ref[c, :, 0:LANES] + pv[:, 0:LANES]
            acc_ref[c, :, LANES:2 * LANES] = alpha * acc_ref[c, :, LANES:2 * LANES] + pv[:, LANES:2 * LANES]
            m_ref[c] = m_new

    def trip(tr, carry):
        for u in range(per_trip):
            conv_unit(tr * per_trip + u)
        for su in range(unroll):
            step(tr * unroll + su)
        return carry

    if ntrip == 1:
        trip(0, 0)
    else:
        lax.fori_loop(0, ntrip, trip, 0)

    lam = (jnp.exp(jnp.sum(lq1_ref[...] * lk1_ref[...], axis=-1, keepdims=True))
           - jnp.exp(jnp.sum(lq2_ref[...] * lk2_ref[...], axis=-1, keepdims=True)) + LAMBDA_INIT)
    for c in range(nch):
        o = acc_ref[c, :, 0:LANES] / acc_ref[c, :, LANES:2 * LANES]
        o = o[0:tq, :] - lam * o[tq:2 * tq, :]
        ms = jnp.mean(o * o, axis=-1, keepdims=True)
        o = (o * lax.rsqrt(ms + SUBLN_EPS) * sw_ref[...]) * (1.0 - LAMBDA_INIT)
        o_ref[0, c * tq:(c + 1) * tq, :] = o.astype(o_ref.dtype)


def _attn(q, k, v1, lq1, lk1, lq2, lk2, subln_w, sh, kh, p, tq, tk, nch, bt, rt):
    b, l, _ = q.shape
    nk = l // tk
    nqs = l // (nch * tq)
    nb = l // p
    c = HYENA_WIDTH
    nct = c // LANES
    nfg = (b * N_DIFF_HEADS * nqs * bt) // (b * nct)
    fb = p // nfg
    assert fb % rt == 0 and nfg * nct * (b // bt) == b * N_DIFF_HEADS * nqs
    s5 = sh.reshape(b, nb, 2, p, c)
    k4 = kh.reshape(2 * nb, 2, p, c)

    def cmap(i, h, j):
        s = (i * N_DIFF_HEADS + h) * nqs + j
        return s // (nfg * nct), (s // nfg) % nct, s % nfg

    vec = lambda a: a.reshape(1, -1)
    k_spec = pl.BlockSpec((1, l, LANES), lambda i, h, j: (i, 0, h))
    v_spec = pl.BlockSpec((1, l, 2 * LANES), lambda i, h, j: (i, 0, h))
    q_spec = pl.BlockSpec((1, nch * tq, LANES), lambda i, h, j: (i, j, h))
    hs_spec = pl.BlockSpec((bt, nb, 2, fb, LANES), lambda i, h, j: (cmap(i, h, j)[0], 0, 0, cmap(i, h, j)[2], cmap(i, h, j)[1]))
    hk_spec = pl.BlockSpec((2 * nb, 2, fb, LANES), lambda i, h, j: (0, 0, cmap(i, h, j)[2], cmap(i, h, j)[1]))
    z_spec = pl.BlockSpec((bt, nb + 1, 2, fb, LANES), lambda i, h, j: (cmap(i, h, j)[0], 0, 0, cmap(i, h, j)[2], cmap(i, h, j)[1]))
    at, z = pl.pallas_call(
        functools.partial(_attn_kernel, tq=tq, tk=tk, nk=nk, nch=nch, nb=nb, rt=rt, unroll=min(16, nk)),
        grid=(b, N_DIFF_HEADS, nqs),
        in_specs=[q_spec, k_spec, v_spec,
                  _const_spec((1, DIFF_HEAD_DIM)), _const_spec((1, DIFF_HEAD_DIM)),
                  _const_spec((1, DIFF_HEAD_DIM)), _const_spec((1, DIFF_HEAD_DIM)),
                  _const_spec((1, DIFF_V_DIM)), hs_spec, hk_spec],
        out_specs=[q_spec, z_spec],
        out_shape=[jax.ShapeDtypeStruct((b, l, ATTN_WIDTH), BF16),
                   jax.ShapeDtypeStruct((b, nb + 1, 2, p, c), F32)],
        scratch_shapes=[pltpu.VMEM((nch, 2 * tq, LANES), BF16),
                        pltpu.VMEM((nch, 2 * tq, LANES), F32),
                        pltpu.VMEM((nch, 2 * tq, 2 * LANES), F32),
                        pltpu.VMEM((bt, nb, fb, LANES), F32),
                        pltpu.VMEM((2 * nb, fb, LANES), F32)],
        compiler_params=_cparams(("parallel", "parallel", "parallel")),
        name="attn",
    )(q, k, v1, vec(lq1), vec(lk1), vec(lq2), vec(lk2), vec(subln_w), s5, k4)
    return at, z.reshape(b, nb + 1, 2 * p, c)


def _post_kernel(x_ref, hy_ref, at_ref, mod_ref, woh_ref, woa_ref, n2_ref, w1_ref, w2_ref, fw_ref, o_ref, *, ffc):
    x = x_ref[0]
    g1 = mod_ref[0, 2:3, :]
    sh2 = mod_ref[0, 3:4, :]
    sc2 = mod_ref[0, 4:5, :]
    g2 = mod_ref[0, 5:6, :]
    mix = (jnp.dot(hy_ref[0], woh_ref[...], preferred_element_type=F32)
           + jnp.dot(at_ref[0], woa_ref[...], preferred_element_type=F32))
    x1 = x + g1 * mix
    ms = jnp.mean(x1 * x1, axis=-1, keepdims=True)
    hb = ((x1 * lax.rsqrt(ms + NORM_EPS) * n2_ref[...]) * (1.0 + sc2) + sh2).astype(BF16)
    acc = jnp.zeros_like(x1)
    for c0 in range(0, D_FF, ffc):
        hc = jnp.dot(hb, w1_ref[:, c0:c0 + ffc], preferred_element_type=F32)
        hc = jnp.square(jnp.maximum(hc, 0.0))
        acc = acc + jnp.dot(hc.astype(BF16), w2_ref[c0:c0 + ffc, :], preferred_element_type=F32)
    x2 = x1 + g2 * acc
    ms2 = jnp.mean(x2 * x2, axis=-1, keepdims=True)
    o_ref[0] = x2 * lax.rsqrt(ms2 + NORM_EPS) * fw_ref[...]


def _post(x, hy, at, mod, wo_bf, norm2_w, w1_bf, w2_bf, final_w, tm):
    b, l, d = x.shape
    tok = lambda w: pl.BlockSpec((1, tm, w), lambda i, j: (i, j, 0))
    return pl.pallas_call(
        functools.partial(_post_kernel, ffc=1024),
        grid=(b, l // tm),
        in_specs=[tok(d), tok(HYENA_WIDTH), tok(ATTN_WIDTH),
                  pl.BlockSpec((1, 6, d), lambda i, j: (i, 0, 0)),
                  _const_spec((HYENA_WIDTH, d)), _const_spec((ATTN_WIDTH, d)),
                  _const_spec((1, d)), _const_spec((d, D_FF)), _const_spec((D_FF, d)), _const_spec((1, d))],
        out_specs=tok(d),
        out_shape=jax.ShapeDtypeStruct((b, l, d), F32),
        compiler_params=_cparams(("parallel", "parallel")),
        name="post",
    )(x, hy, at, mod, wo_bf[:HYENA_WIDTH], wo_bf[HYENA_WIDTH:], norm2_w.reshape(1, d), w1_bf, w2_bf,
      final_w.reshape(1, d))


def _tiles(b, l):
    tm = min(512, l)
    nsub = 2 if l >= 2 * tm else 1
    p = 1024 if l >= 8192 else min(512, l // 2)
    nch = 4
    tq = min(256, l // nch)
    tk = min(512, l)
    bt = math.gcd(b, 2)
    rt = (8 * SUBLANES) // bt
    return dict(tm=tm, nsub=nsub, p=p, tq=tq, tk=tk, nch=nch, bt=bt, rt=rt)


def _encoder(x, mod, w, tables):
    b, l, d = x.shape
    t = _tiles(b, l)
    p = t["p"]
    fwd2_bf, inv2_bf = tables["fwd2"], tables["inv2"]
    cos_t, sin_t = _rope_tables(l)
    sh, x0, s, q, k, v = _inproj(x, mod, w["norm1_w"], w["w_in"], cos_t, sin_t, w["conv_w"], w["conv_b"],
                                 fwd2_bf[0], t["tm"], t["nsub"], p)
    kh = _filter_blocks(l, p, w["filt_w1"], w["filt_b1"], w["filt_w2"], w["filt_b2"], w["filt_w3"], w["filt_b3"],
                        w["filt_w4"], w["filt_freq"], fwd2_bf)
    at, zh = _attn(q, k, v, w["lambda_q1"], w["lambda_k1"], w["lambda_q2"], w["lambda_k2"], w["subln_w"],
                   sh, kh, p, t["tq"], t["tk"], t["nch"], t["bt"], t["rt"])
    hy = _hyout(zh, inv2_bf, x0, s, w["hyena_bias"], w["hyena_norm_w"], p)
    return _post(x, hy, at, mod, w["w_out"], w["norm2_w"], w["w_mlp1"], w["w_mlp2"], w["final_w"], t["tm"])


def kernel(x_prompt, x_sample, c_prompt, c_sample, w_ada, b_ada, norm1_w, w_in, conv_w, conv_b, filt_w1, filt_b1, filt_w2, filt_b2, filt_w3, filt_b3, filt_w4, filt_freq, hyena_bias, hyena_norm_w, lambda_q1, lambda_k1, lambda_q2, lambda_k2, subln_w, w_out, norm2_w, w_mlp1, w_mlp2, final_w):
    assert w_ada.shape[0] == 1, "single-layer encoder"
    w = dict(norm1_w=norm1_w[0], w_in=w_in[0].astype(BF16), conv_w=conv_w[0], conv_b=conv_b[0],
             filt_w1=filt_w1[0], filt_b1=filt_b1[0], filt_w2=filt_w2[0], filt_b2=filt_b2[0],
             filt_w3=filt_w3[0], filt_b3=filt_b3[0], filt_w4=filt_w4[0], filt_freq=filt_freq[0],
             hyena_bias=hyena_bias[0], hyena_norm_w=hyena_norm_w[0], lambda_q1=lambda_q1[0],
             lambda_k1=lambda_k1[0], lambda_q2=lambda_q2[0], lambda_k2=lambda_k2[0], subln_w=subln_w[0],
             w_out=w_out[0].astype(BF16), norm2_w=norm2_w[0], w_mlp1=w_mlp1[0].astype(BF16),
             w_mlp2=w_mlp2[0].astype(BF16), final_w=final_w)
    nbp = c_prompt.shape[0]
    mod = _ada(jnp.concatenate([c_prompt, c_sample], axis=0), w_ada[0], b_ada[0])
    mod = mod.reshape(mod.shape[0], 6, D_MODEL)
    outs = []
    for x, m in ((x_prompt, mod[:nbp]), (x_sample, mod[nbp:])):
        p = _tiles(*x.shape[:2])["p"]
        fwd2, inv2 = _dft_mats(p)
        tables = dict(fwd2=fwd2.astype(BF16), inv2=inv2.astype(BF16))
        outs.append(_encoder(x, m, w, tables))
    return tuple(outs)
```

```python
import functools
import math

import numpy as np
import jax
import jax.numpy as jnp
from jax import lax
from jax.experimental import pallas as pl
from jax.experimental.pallas import tpu as pltpu

F32 = jnp.float32
BF16 = jnp.bfloat16

D_MODEL = 1024
HYENA_WIDTH = 512
ATTN_WIDTH = 512
N_DIFF_HEADS = 4
DIFF_HEAD_DIM = 64
DIFF_V_DIM = 2 * DIFF_HEAD_DIM
ROT_DIM = DIFF_HEAD_DIM // 4
ROPE_THETA = 500000.0
D_IN = 3 * HYENA_WIDTH + 3 * ATTN_WIDTH
D_FF = 4 * D_MODEL
FILTER_EMB = 33
FILTER_ORDER = 64
FAST_DECAY_PCT = 0.3
SLOW_DECAY_PCT = 1.5
DECAY_TARGET = 1e-2
NORM_EPS = 1e-6
SUBLN_EPS = 1e-5
LAMBDA_INIT = 0.8 - 0.6 * math.exp(-0.3 * 0)

LANES = 128
SUBLANES = 8
HALO = 2 * SUBLANES
VMEM_LIMIT = 56 * 1024 * 1024

HIGHEST = lax.Precision.HIGHEST


def _cparams(sem):
    return pltpu.CompilerParams(dimension_semantics=sem, vmem_limit_bytes=VMEM_LIMIT)


def _const_spec(shape):
    nd = len(shape)
    return pl.BlockSpec(shape, lambda *_: (0,) * nd, pipeline_mode=pl.Buffered(1))


def _ada_kernel(c_ref, w_ref, b_ref, o_ref):
    c = c_ref[...]
    s = c * jax.nn.sigmoid(c)
    o_ref[...] = jnp.dot(s.astype(BF16), w_ref[...].astype(BF16), preferred_element_type=F32) + b_ref[...]


def _ada(c, w_ada, b_ada):
    nb_, d = c.shape
    n = w_ada.shape[1]
    tn = 1024
    return pl.pallas_call(
        _ada_kernel,
        grid=(n // tn,),
        in_specs=[pl.BlockSpec((nb_, d), lambda j: (0, 0)),
                  pl.BlockSpec((d, tn), lambda j: (0, j)),
                  pl.BlockSpec((1, tn), lambda j: (0, j))],
        out_specs=pl.BlockSpec((nb_, tn), lambda j: (0, j)),
        out_shape=jax.ShapeDtypeStruct((nb_, n), F32),
        compiler_params=_cparams(("parallel",)),
        name="ada",
    )(c, w_ada, b_ada.reshape(1, n))


def _inproj_kernel(x_ref, xp_ref, xn_ref, mod_ref, nw_ref, w_ref, cos_ref, sin_ref, cw_ref, cb_ref, fwd_ref,
                   sh_ref, x0_ref, s_ref, q_ref, k_ref, v_ref, *, nt, tm, r):
    j = pl.program_id(1)
    nsub = x_ref.shape[1] // tm
    c = HYENA_WIDTH
    hw3 = 3 * c
    shift = mod_ref[0, 0:1, :]
    scale = mod_ref[0, 1:2, :]

    def normmod(x):
        ms = jnp.mean(x * x, axis=-1, keepdims=True)
        return ((x * lax.rsqrt(ms + NORM_EPS) * nw_ref[...]) * (1.0 + scale) + shift).astype(BF16)

    hp = jnp.where(j > 0, normmod(xp_ref[0]), jnp.zeros((HALO, x_ref.shape[2]), BF16))
    hn = jnp.where(j < nt - 1, normmod(xn_ref[0]), jnp.zeros((HALO, x_ref.shape[2]), BF16))
    he = jnp.concatenate([hp] + [normmod(x_ref[0, u * tm:(u + 1) * tm, :]) for u in range(nsub)] + [hn], axis=0)

    hl = lax.broadcasted_iota(jnp.int32, (tm, LANES), 1) % DIFF_HEAD_DIM
    lo_half = hl < ROT_DIM // 2
    hi_half = jnp.logical_and(hl >= ROT_DIM // 2, hl < ROT_DIM)
    parts = {}
    for u in range(nsub):
        rows = slice(u * tm, (u + 1) * tm)
        hs = he[u * tm:(u + 1) * tm + 2 * HALO]
        hb = hs[HALO:HALO + tm]
        cvs = []
        for c0 in range(0, hw3, c):
            ue = jnp.dot(hs, w_ref[:, c0:c0 + c], preferred_element_type=F32)
            cvs.append(ue[HALO - 1:HALO - 1 + tm] * cw_ref[0:1, c0:c0 + c] + ue[HALO:HALO + tm] * cw_ref[1:2, c0:c0 + c]
                       + ue[HALO + 1:HALO + 1 + tm] * cw_ref[2:3, c0:c0 + c] + cb_ref[:, c0:c0 + c])
        s = (cvs[1] * cvs[2]).astype(BF16)
        x0_ref[0, rows, :] = cvs[0].astype(BF16)
        s_ref[0, rows, :] = s
        blk, kp = divmod(u, r)
        part = jnp.dot(fwd_ref[:, kp * tm:(kp + 1) * tm], s, preferred_element_type=F32)
        parts[blk] = part if kp == 0 else parts[blk] + part

        cos_t = cos_ref[rows, :]
        sin_t = sin_ref[rows, :]
        sa = jnp.where(lo_half, -sin_t, 0.0)
        sb = jnp.where(hi_half, sin_t, 0.0)

        def rope(xg):
            return (xg * cos_t + pltpu.roll(xg, LANES - ROT_DIM // 2, axis=1) * sa
                    + pltpu.roll(xg, ROT_DIM // 2, axis=1) * sb)

        pq = jnp.dot(hb, w_ref[:, hw3:hw3 + ATTN_WIDTH], preferred_element_type=F32)
        pk = jnp.dot(hb, w_ref[:, hw3 + ATTN_WIDTH:hw3 + 2 * ATTN_WIDTH], preferred_element_type=F32)
        for g in range(ATTN_WIDTH // LANES):
            sl = slice(g * LANES, (g + 1) * LANES)
            q_ref[0, rows, sl] = (rope(pq[:, sl]) * (math.log2(math.e) * DIFF_HEAD_DIM ** -0.5)).astype(BF16)
            k_ref[0, rows, sl] = rope(pk[:, sl]).astype(BF16)
        pv = jnp.dot(hb, w_ref[:, hw3 + 2 * ATTN_WIDTH:], preferred_element_type=F32).astype(BF16)
        ones = jnp.ones((tm, DIFF_V_DIM), BF16)
        for g in range(N_DIFF_HEADS):
            v_ref[0, rows, 2 * g * DIFF_V_DIM:(2 * g + 1) * DIFF_V_DIM] = pv[:, g * DIFF_V_DIM:(g + 1) * DIFF_V_DIM]
            v_ref[0, rows, (2 * g + 1) * DIFF_V_DIM:(2 * g + 2) * DIFF_V_DIM] = ones
    for blk, part in parts.items():
        sh_ref[0, blk] = part


def _inproj(x, mod, norm_w, w_in_bf, cos_t, sin_t, conv_w, conv_b, fwd_bf, tm, nsub, p):
    b, l, d = x.shape
    c = HYENA_WIDTH
    hw3 = 3 * c
    tmb = nsub * tm
    nt = l // tmb
    r = p // tm
    assert nsub % r == 0
    cbs = nsub // r
    rh = tmb // HALO
    nrowh = l // HALO
    tok = lambda w: pl.BlockSpec((1, tmb, w), lambda i, j: (i, j, 0))
    tab = pl.BlockSpec((tmb, LANES), lambda i, j: (j, 0))
    return pl.pallas_call(
        functools.partial(_inproj_kernel, nt=nt, tm=tm, r=r),
        grid=(b, nt),
        in_specs=[tok(d),
                  pl.BlockSpec((1, HALO, d), lambda i, j: (i, jnp.maximum(j * rh - 1, 0), 0)),
                  pl.BlockSpec((1, HALO, d), lambda i, j: (i, jnp.minimum((j + 1) * rh, nrowh - 1), 0)),
                  pl.BlockSpec((1, 6, d), lambda i, j: (i, 0, 0)),
                  _const_spec((1, d)),
                  _const_spec((d, D_IN)),
                  tab, tab,
                  _const_spec((3, hw3)), _const_spec((1, hw3)),
                  _const_spec((2 * p, p))],
        out_specs=[pl.BlockSpec((1, cbs, 2 * p, c), lambda i, j: (i, j, 0, 0)),
                   tok(c), tok(c), tok(ATTN_WIDTH), tok(ATTN_WIDTH), tok(2 * ATTN_WIDTH)],
        out_shape=[jax.ShapeDtypeStruct((b, l // p, 2 * p, c), F32),
                   jax.ShapeDtypeStruct((b, l, c), BF16),
                   jax.ShapeDtypeStruct((b, l, c), BF16),
                   jax.ShapeDtypeStruct((b, l, ATTN_WIDTH), BF16),
                   jax.ShapeDtypeStruct((b, l, ATTN_WIDTH), BF16),
                   jax.ShapeDtypeStruct((b, l, 2 * ATTN_WIDTH), BF16)],
        compiler_params=_cparams(("parallel", "parallel")),
        name="inproj",
    )(x, x, x, mod, norm_w.reshape(1, d), w_in_bf, cos_t, sin_t, conv_w, conv_b.reshape(1, hw3), fwd_bf)


def _rope_tables(l):
    inv_freq = ROPE_THETA ** (-jnp.arange(0, ROT_DIM, 2, dtype=F32) / ROT_DIM)
    hl = np.arange(LANES) % DIFF_HEAD_DIM
    lane_freq = jnp.where(hl < ROT_DIM, inv_freq[hl % (ROT_DIM // 2)], 0.0)
    ang = jnp.arange(l, dtype=F32)[:, None] * lane_freq[None, :]
    return jnp.cos(ang), jnp.sin(ang)


def _dft_mats(p):
    f = np.arange(p, dtype=np.float64)
    n = np.arange(p, dtype=np.float64)
    th = np.pi * (2.0 * f[:, None] + 1.0) * n[None, :] / (2.0 * p)
    fwd = np.concatenate([np.cos(th), -np.sin(th)], axis=0)
    n2 = np.arange(2 * p, dtype=np.float64)
    th2 = np.pi * (2.0 * f[None, :] + 1.0) * n2[:, None] / (2.0 * p)
    inv = np.concatenate([np.cos(th2), -np.sin(th2)], axis=1) / p
    fwd2 = np.stack([fwd, fwd[:, ::-1]], axis=0).astype(np.float32)
    inv2 = np.stack([inv[:p], inv[p:]], axis=0).astype(np.float32)
    return jnp.asarray(fwd2), jnp.asarray(inv2)


def _filt_kernel(fb_ref, w1t_ref, w1c_ref, w1s_ref, b1_ref, w2_ref, b2_ref, w3_ref, b3_ref, w4h_ref, w4b_ref,
                 fr_ref, ad_ref, mat_ref, o_ref, *, l, p, nb):
    j = pl.program_id(0)
    neg = j < nb
    e = jnp.where(neg, nb - 1 - j, j - nb)
    base = e * p + jnp.where(neg, 1, 0)
    posl = (base + lax.broadcasted_iota(jnp.int32, (1, p), 1)).astype(F32)
    tl = posl / float(l - 1)
    wl = (2.0 * math.pi) * posl / float(l)
    ang = fb_ref[...] * wl
    fr = fr_ref[...]
    dot = functools.partial(jnp.dot, precision=HIGHEST, preferred_element_type=F32)
    z1 = w1t_ref[...] * tl + dot(w1c_ref[...], jnp.cos(ang)) - dot(w1s_ref[...], jnp.sin(ang))
    h = jnp.sin(fr * (z1 + b1_ref[...]))
    h = jnp.sin(fr * (dot(w2_ref[...], h) + b2_ref[...]))
    h = jnp.sin(fr * (dot(w3_ref[...], h) + b3_ref[...]))
    row = lax.broadcasted_iota(jnp.int32, (p, 1), 0)
    pos = base + row
    t = pos.astype(F32) / float(l - 1)
    kap = lax.dot_general(h, w4h_ref[...], (((0,), (0,)), ((), ())), precision=HIGHEST,
                          preferred_element_type=F32) * jnp.exp(-t * ad_ref[...])
    kap = jnp.where(pos < l, kap, 0.0)
    hb0 = jnp.sum(h[:, 0:1] * w4b_ref[...], axis=0, keepdims=True)
    kap = kap + jnp.where(jnp.logical_and(row == 0, j == nb), hb0, 0.0)
    o_ref[0] = jnp.dot(mat_ref[0], kap.astype(BF16), preferred_element_type=F32)


def _filter_blocks(l, p, fw1, fb1, fw2, fb2, fw3, fb3, fw4, ffreq, fwd2_bf):
    nb = l // p
    c = HYENA_WIDTH
    bands = (FILTER_EMB - 1) // 2
    fbands = jnp.linspace(1e-4, bands - 1, bands, dtype=F32)[:, None]
    col = lambda a: a.reshape(-1, 1)
    min_decay = math.log(DECAY_TARGET) / SLOW_DECAY_PCT
    max_decay = math.log(DECAY_TARGET) / FAST_DECAY_PCT
    absd = jnp.abs(jnp.linspace(min_decay, max_decay, c, dtype=F32))[None, :]
    half = lambda j: jnp.where(j < nb, 1, 0)
    args = (fbands, col(fw1[0]), fw1[1:1 + bands].T, fw1[1 + bands:].T, col(fb1), fw2.T, col(fb2),
            fw3.T, col(fb3), fw4, fw4[:, c:], col(ffreq), absd, fwd2_bf)
    specs = [_const_spec(a.shape) for a in args]
    specs[9] = pl.BlockSpec((FILTER_ORDER, c), lambda j: (0, half(j)))
    specs[13] = pl.BlockSpec((1, 2 * p, p), lambda j: (half(j), 0, 0))
    return pl.pallas_call(
        functools.partial(_filt_kernel, l=l, p=p, nb=nb),
        grid=(2 * nb,),
        in_specs=specs,
        out_specs=pl.BlockSpec((1, 2 * p, c), lambda j: (j, 0, 0)),
        out_shape=jax.ShapeDtypeStruct((2 * nb, 2 * p, c), F32),
        compiler_params=_cparams(("parallel",)),
        name="filt",
    )(*args)


def _hyout_kernel(z_ref, inv_ref, x0_ref, s_ref, bias_ref, nw_ref, o_ref, tail_ref):
    g = pl.program_id(1)
    zbs = [z_ref[i, 0].astype(BF16) for i in range(z_ref.shape[0])]

    @pl.when(g > 0)
    def _():
        for i, zb in enumerate(zbs):
            y = tail_ref[i] + jnp.dot(inv_ref[0], zb, preferred_element_type=F32)
            y = x0_ref[i].astype(F32) * (y + s_ref[i].astype(F32) * bias_ref[...])
            ms = jnp.mean(y * y, axis=-1, keepdims=True)
            o_ref[i] = (y * lax.rsqrt(ms + NORM_EPS) * nw_ref[...]).astype(o_ref.dtype)

    @pl.when(g < pl.num_programs(1) - 1)
    def _():
        for i, zb in enumerate(zbs):
            tail_ref[i] = jnp.dot(inv_ref[1], zb, preferred_element_type=F32)


def _hyout(zh, inv2_bf, x0, s, hbias, hnorm_w, p):
    b, l, c = x0.shape
    nb = l // p
    bb = math.gcd(b, 2)
    tok = pl.BlockSpec((bb, p, c), lambda i, g: (i, jnp.maximum(g - 1, 0), 0))
    return pl.pallas_call(
        _hyout_kernel,
        grid=(b // bb, nb + 1),
        in_specs=[pl.BlockSpec((bb, 1, 2 * p, c), lambda i, g: (i, g, 0, 0)),
                  _const_spec((2, p, 2 * p)),
                  tok, tok, _const_spec((1, c)), _const_spec((1, c))],
        out_specs=tok,
        out_shape=jax.ShapeDtypeStruct((b, l, c), BF16),
        scratch_shapes=[pltpu.VMEM((bb, p, c), F32)],
        compiler_params=_cparams(("parallel", "arbitrary")),
        name="hyout",
    )(zh, inv2_bf, x0, s, hbias.reshape(1, c), hnorm_w.reshape(1, c))


def _attn_kernel(q_ref, k_ref, v_ref, lq1_ref, lk1_ref, lq2_ref, lk2_ref, sw_ref, hs_ref, hk_ref, o_ref, z_ref,
                 qs_ref, m_ref, acc_ref, ss_ref, ks_ref, *, tq, tk, nk, nch, nb, rt, unroll):
    bt, fb = hs_ref.shape[0], hs_ref.shape[3]
    ss_ref[...] = hs_ref[:, :, 0] + hs_ref[:, :, 1]
    ks_ref[...] = hk_ref[:, 0] + hk_ref[:, 1]
    units = (fb // rt) * (nb + 1)
    ntrip = nk // unroll
    per_trip = -(-units // ntrip)

    def conv_unit(idx):
        if isinstance(idx, int):
            tile, e = divmod(min(idx, units - 1), nb + 1)
            rows = pl.ds(tile * rt, rt)
        else:
            idx = jnp.minimum(idx, units - 1)
            tile = lax.div(idx, nb + 1)
            e = lax.rem(idx, nb + 1)
            rows = pl.ds(pl.multiple_of(tile * rt, rt), rt)
        t1 = t2 = t3 = jnp.zeros((bt, rt, LANES), F32)
        for a in range(nb):
            j = e - 1 - a + nb
            t1 = t1 + hk_ref[j, 0, rows, :][None] * hs_ref[:, a, 0, rows, :]
            t2 = t2 + hk_ref[j, 1, rows, :][None] * hs_ref[:, a, 1, rows, :]
            t3 = t3 + ks_ref[j, rows, :][None] * ss_ref[:, a, rows, :]
        z_ref[:, e, 0, rows, :] = t1 - t2
        z_ref[:, e, 1, rows, :] = t3 - t1 - t2

    lane = lax.broadcasted_iota(jnp.int32, (tq, LANES), 1)
    for c in range(nch):
        q = q_ref[0, c * tq:(c + 1) * tq, :]
        zero = jnp.zeros_like(q)
        qs_ref[c, 0:tq, :] = jnp.where(lane < DIFF_HEAD_DIM, q, zero)
        qs_ref[c, tq:2 * tq, :] = jnp.where(lane >= DIFF_HEAD_DIM, q, zero)
    m_ref[...] = jnp.full(m_ref.shape, -jnp.inf, F32)
    acc_ref[...] = jnp.zeros(acc_ref.shape, F32)
    nt = tk // LANES

    def step(i):
        off = i * tk if isinstance(i, int) else pl.multiple_of(i * tk, tk)
        kb = k_ref[0, pl.ds(off, tk), :]
        vb = v_ref[0, pl.ds(off, tk), :]
        for c in range(nch):
            s = lax.dot_general(qs_ref[c], kb, (((1,), (1,)), ((), ())), preferred_element_type=F32)
            st = [s[:, t * LANES:(t + 1) * LANES] for t in range(nt)]
            mt = st[0]
            for t in range(1, nt):
                mt = jnp.maximum(mt, st[t])
            m_prev = m_ref[c]
            m_new = jnp.maximum(m_prev, jnp.max(mt, axis=-1, keepdims=True))
            alpha = jnp.exp2(m_prev - m_new)
            pexp = jnp.concatenate([jnp.exp2(st[t] - m_new).astype(BF16) for t in range(nt)], axis=1)
            pv = jnp.dot(pexp, vb, preferred_element_type=F32)
            acc_ref[c, :, 0:LANES] = alpha * acc_ref[c, :, 0:LANES] + pv[:, 0:LANES]
            acc_ref[c, :, LANES:2 * LANES] = alpha * acc_ref[c, :, LANES:2 * LANES] + pv[:, LANES:2 * LANES]
            m_ref[c] = m_new

    def trip(tr, carry):
        for u in range(per_trip):
            conv_unit(tr * per_trip + u)
        for su in range(unroll):
            step(tr * unroll + su)
        return carry

    if ntrip == 1:
        trip(0, 0)
    else:
        lax.fori_loop(0, ntrip, trip, 0)

    lam = (jnp.exp(jnp.sum(lq1_ref[...] * lk1_ref[...], axis=-1, keepdims=True))
           - jnp.exp(jnp.sum(lq2_ref[...] * lk2_ref[...], axis=-1, keepdims=True)) + LAMBDA_INIT)
    for c in range(nch):
        o = acc_ref[c, :, 0:LANES] / acc_ref[c, :, LANES:2 * LANES]
        o = o[0:tq, :] - lam * o[tq:2 * tq, :]
        ms = jnp.mean(o * o, axis=-1, keepdims=True)
        o = (o * lax.rsqrt(ms + SUBLN_EPS) * sw_ref[...]) * (1.0 - LAMBDA_INIT)
        o_ref[0, c * tq:(c + 1) * tq, :] = o.astype(o_ref.dtype)


def _attn(q, k, v1, lq1, lk1, lq2, lk2, subln_w, sh, kh, p, tq, tk, nch, bt, rt):
    b, l, _ = q.shape
    nk = l // tk
    nqs = l // (nch * tq)
    nb = l // p
    c = HYENA_WIDTH
    nct = c // LANES
    nfg = (b * N_DIFF_HEADS * nqs * bt) // (b * nct)
    fb = p // nfg
    assert fb % rt == 0 and nfg * nct * (b // bt) == b * N_DIFF_HEADS * nqs
    s5 = sh.reshape(b, nb, 2, p, c)
    k4 = kh.reshape(2 * nb, 2, p, c)

    def cmap(i, h, j):
        s = (i * N_DIFF_HEADS + h) * nqs + j
        return s // (nfg * nct), (s // nfg) % nct, s % nfg

    vec = lambda a: a.reshape(1, -1)
    k_spec = pl.BlockSpec((1, l, LANES), lambda i, h, j: (i, 0, h))
    v_spec = pl.BlockSpec((1, l, 2 * LANES), lambda i, h, j: (i, 0, h))
    q_spec = pl.BlockSpec((1, nch * tq, LANES), lambda i, h, j: (i, j, h))
    hs_spec = pl.BlockSpec((bt, nb, 2, fb, LANES), lambda i, h, j: (cmap(i, h, j)[0], 0, 0, cmap(i, h, j)[2], cmap(i, h, j)[1]))
    hk_spec = pl.BlockSpec((2 * nb, 2, fb, LANES), lambda i, h, j: (0, 0, cmap(i, h, j)[2], cmap(i, h, j)[1]))
    z_spec = pl.BlockSpec((bt, nb + 1, 2, fb, LANES), lambda i, h, j: (cmap(i, h, j)[0], 0, 0, cmap(i, h, j)[2], cmap(i, h, j)[1]))
    at, z = pl.pallas_call(
        functools.partial(_attn_kernel, tq=tq, tk=tk, nk=nk, nch=nch, nb=nb, rt=rt, unroll=min(32, nk)),
        grid=(b, N_DIFF_HEADS, nqs),
        in_specs=[q_spec, k_spec, v_spec,
                  _const_spec((1, DIFF_HEAD_DIM)), _const_spec((1, DIFF_HEAD_DIM)),
                  _const_spec((1, DIFF_HEAD_DIM)), _const_spec((1, DIFF_HEAD_DIM)),
                  _const_spec((1, DIFF_V_DIM)), hs_spec, hk_spec],
        out_specs=[q_spec, z_spec],
        out_shape=[jax.ShapeDtypeStruct((b, l, ATTN_WIDTH), BF16),
                   jax.ShapeDtypeStruct((b, nb + 1, 2, p, c), F32)],
        scratch_shapes=[pltpu.VMEM((nch, 2 * tq, LANES), BF16),
                        pltpu.VMEM((nch, 2 * tq, LANES), F32),
                        pltpu.VMEM((nch, 2 * tq, 2 * LANES), F32),
                        pltpu.VMEM((bt, nb, fb, LANES), F32),
                        pltpu.VMEM((2 * nb, fb, LANES), F32)],
        compiler_params=_cparams(("parallel", "parallel", "parallel")),
        name="attn",
    )(q, k, v1, vec(lq1), vec(lk1), vec(lq2), vec(lk2), vec(subln_w), s5, k4)
    return at, z.reshape(b, nb + 1, 2 * p, c)


def _post_kernel(x_ref, hy_ref, at_ref, mod_ref, woh_ref, woa_ref, n2_ref, w1_ref, w2_ref, fw_ref, o_ref, *, ffc):
    x = x_ref[0]
    g1 = mod_ref[0, 2:3, :]
    sh2 = mod_ref[0, 3:4, :]
    sc2 = mod_ref[0, 4:5, :]
    g2 = mod_ref[0, 5:6, :]
    mix = (jnp.dot(hy_ref[0], woh_ref[...], preferred_element_type=F32)
           + jnp.dot(at_ref[0], woa_ref[...], preferred_element_type=F32))
    x1 = x + g1 * mix
    ms = jnp.mean(x1 * x1, axis=-1, keepdims=True)
    hb = ((x1 * lax.rsqrt(ms + NORM_EPS) * n2_ref[...]) * (1.0 + sc2) + sh2).astype(BF16)
    acc = jnp.zeros_like(x1)
    for c0 in range(0, D_FF, ffc):
        hc = jnp.dot(hb, w1_ref[:, c0:c0 + ffc], preferred_element_type=F32)
        hc = jnp.square(jnp.maximum(hc, 0.0))
        acc = acc + jnp.dot(hc.astype(BF16), w2_ref[c0:c0 + ffc, :], preferred_element_type=F32)
    x2 = x1 + g2 * acc
    ms2 = jnp.mean(x2 * x2, axis=-1, keepdims=True)
    o_ref[0] = x2 * lax.rsqrt(ms2 + NORM_EPS) * fw_ref[...]


def _post(x, hy, at, mod, wo_bf, norm2_w, w1_bf, w2_bf, final_w, tm):
    b, l, d = x.shape
    tok = lambda w: pl.BlockSpec((1, tm, w), lambda i, j: (i, j, 0))
    return pl.pallas_call(
        functools.partial(_post_kernel, ffc=1024),
        grid=(b, l // tm),
        in_specs=[tok(d), tok(HYENA_WIDTH), tok(ATTN_WIDTH),
                  pl.BlockSpec((1, 6, d), lambda i, j: (i, 0, 0)),
                  _const_spec((HYENA_WIDTH, d)), _const_spec((ATTN_WIDTH, d)),
                  _const_spec((1, d)), _const_spec((d, D_FF)), _const_spec((D_FF, d)), _const_spec((1, d))],
        out_specs=tok(d),
        out_shape=jax.ShapeDtypeStruct((b, l, d), F32),
        compiler_params=_cparams(("parallel", "parallel")),
        name="post",
    )(x, hy, at, mod, wo_bf[:HYENA_WIDTH], wo_bf[HYENA_WIDTH:], norm2_w.reshape(1, d), w1_bf, w2_bf,
      final_w.reshape(1, d))


def _tiles(b, l):
    tm = min(512, l)
    nsub = 2 if l >= 2 * tm else 1
    p = 1024 if l >= 8192 else min(512, l // 2)
    nch = 8 if l <= 2048 else 4
    tq = min(256, l // nch)
    tk = min(512, l)
    bt = math.gcd(b, 2)
    rt = (8 * SUBLANES) // bt
    return dict(tm=tm, nsub=nsub, p=p, tq=tq, tk=tk, nch=nch, bt=bt, rt=rt)


def _encoder(x, mod, w, tables):
    b, l, d = x.shape
    t = _tiles(b, l)
    p = t["p"]
    fwd2_bf, inv2_bf = tables["fwd2"], tables["inv2"]
    cos_t, sin_t = _rope_tables(l)
    sh, x0, s, q, k, v = _inproj(x, mod, w["norm1_w"], w["w_in"], cos_t, sin_t, w["conv_w"], w["conv_b"],
                                 fwd2_bf[0], t["tm"], t["nsub"], p)
    kh = _filter_blocks(l, p, w["filt_w1"], w["filt_b1"], w["filt_w2"], w["filt_b2"], w["filt_w3"], w["filt_b3"],
                        w["filt_w4"], w["filt_freq"], fwd2_bf)
    at, zh = _attn(q, k, v, w["lambda_q1"], w["lambda_k1"], w["lambda_q2"], w["lambda_k2"], w["subln_w"],
                   sh, kh, p, t["tq"], t["tk"], t["nch"], t["bt"], t["rt"])
    hy = _hyout(zh, inv2_bf, x0, s, w["hyena_bias"], w["hyena_norm_w"], p)
    return _post(x, hy, at, mod, w["w_out"], w["norm2_w"], w["w_mlp1"], w["w_mlp2"], w["final_w"], t["tm"])


def kernel(x_prompt, x_sample, c_prompt, c_sample, w_ada, b_ada, norm1_w, w_in, conv_w, conv_b, filt_w1, filt_b1, filt_w2, filt_b2, filt_w3, filt_b3, filt_w4, filt_freq, hyena_bias, hyena_norm_w, lambda_q1, lambda_k1, lambda_q2, lambda_k2, subln_w, w_out, norm2_w, w_mlp1, w_mlp2, final_w):
    assert w_ada.shape[0] == 1, "single-layer encoder"
    w = dict(norm1_w=norm1_w[0], w_in=w_in[0].astype(BF16), conv_w=conv_w[0], conv_b=conv_b[0],
             filt_w1=filt_w1[0], filt_b1=filt_b1[0], filt_w2=filt_w2[0], filt_b2=filt_b2[0],
             filt_w3=filt_w3[0], filt_b3=filt_b3[0], filt_w4=filt_w4[0], filt_freq=filt_freq[0],
             hyena_bias=hyena_bias[0], hyena_norm_w=hyena_norm_w[0], lambda_q1=lambda_q1[0],
             lambda_k1=lambda_k1[0], lambda_q2=lambda_q2[0], lambda_k2=lambda_k2[0], subln_w=subln_w[0],
             w_out=w_out[0].astype(BF16), norm2_w=norm2_w[0], w_mlp1=w_mlp1[0].astype(BF16),
             w_mlp2=w_mlp2[0].astype(BF16), final_w=final_w)
    nbp = c_prompt.shape[0]
    mod = _ada(jnp.concatenate([c_prompt, c_sample], axis=0), w_ada[0], b_ada[0])
    mod = mod.reshape(mod.shape[0], 6, D_MODEL)
    outs = []
    for x, m in ((x_prompt, mod[:nbp]), (x_sample, mod[nbp:])):
        p = _tiles(*x.shape[:2])["p"]
        fwd2, inv2 = _dft_mats(p)
        tables = dict(fwd2=fwd2.astype(BF16), inv2=inv2.astype(BF16))
        outs.append(_encoder(x, m, w, tables))
    return tuple(outs)
```
